```python
import math
import jax
import jax.numpy as jnp
from jax import lax
import numpy as np

D_MODEL = 1024
BATCH = 2
SEQ = 16384
DEPTH = 2
DEC_BATCH = 8
DEC_SEQ = 16
PAST_LEN = 1024

CHUNK = 64
WINDOW = 128
N_WIN_CHUNKS = WINDOW // CHUNK
HEAD_DIM = 64
N_Q_HEADS = 8
N_KV_HEADS = 2
GROUP = N_Q_HEADS // N_KV_HEADS
ATTN_WIDTH = N_Q_HEADS * HEAD_DIM
KV_WIDTH = N_KV_HEADS * HEAD_DIM
POOL_WINDOWS = (2, 4, 8, 16)
N_POOL_GROUPS = 4
POOL_WIDTH = 512
POOL_GROUP_DIM = POOL_WIDTH // N_POOL_GROUPS
POOL_MAXW = max(POOL_WINDOWS)
POOL_HIST = POOL_MAXW - 1
N_BRANCHES = 2
IN_WIDTH = ATTN_WIDTH + 2 * KV_WIDTH + POOL_WIDTH + N_BRANCHES * D_MODEL
D_FF = 2816
CONV_WIDTH = 3
CONV_HIST = CONV_WIDTH - 1
EPS = 1e-6
NEG_INF = -1e30

kernel_name = "hybrid_streaming_swa_pool_convffn_step"


def rms_norm(x, g):
    xf = x.astype(jnp.float32)
    y = xf * lax.rsqrt(jnp.mean(xf * xf, axis=-1, keepdims=True) + EPS)
    return (y * g.astype(jnp.float32)).astype(x.dtype)


def alibi_slopes():
    return jnp.asarray([2.0 ** (-8.0 * (h + 1) / N_Q_HEADS) for h in range(N_Q_HEADS)], dtype=jnp.float32)


def alibi_bias(rel):
    b = -alibi_slopes()[:, None, None] * jnp.abs(rel).astype(jnp.float32)[None]
    return b.reshape(N_KV_HEADS, GROUP, rel.shape[0], rel.shape[1])


def sink_softmax_apply(scores, sinks_b, v, eq):
    m = jnp.maximum(scores.max(-1), sinks_b)
    p = jnp.exp(scores - m[..., None])
    denom = p.sum(-1) + jnp.exp(sinks_b - m)
    p = p / denom[..., None]
    return jnp.einsum(eq, p.astype(v.dtype), v)


def swa_prompt(q, k, v, sinks):
    B, S = q.shape[0], q.shape[1]
    n_c = S // CHUNK
    kb_len = (N_WIN_CHUNKS + 1) * CHUNK
    qc = q.reshape(B, n_c, CHUNK, N_KV_HEADS, GROUP, HEAD_DIM)
    pad = ((0, 0), (WINDOW, 0), (0, 0), (0, 0))
    kp = jnp.pad(k, pad).reshape(B, n_c + N_WIN_CHUNKS, CHUNK, N_KV_HEADS, HEAD_DIM)
    vp = jnp.pad(v, pad).reshape(B, n_c + N_WIN_CHUNKS, CHUNK, N_KV_HEADS, HEAD_DIM)
    kb = jnp.concatenate([kp[:, j:j + n_c] for j in range(N_WIN_CHUNKS + 1)], axis=2)
    vb = jnp.concatenate([vp[:, j:j + n_c] for j in range(N_WIN_CHUNKS + 1)], axis=2)
    scores = jnp.einsum('bcqkgd,bcskd->bckgqs', qc, kb).astype(jnp.float32) * (HEAD_DIM ** -0.5)
    rel = WINDOW + jnp.arange(CHUNK)[:, None] - jnp.arange(kb_len)[None, :]
    key_pos = jnp.arange(n_c)[:, None] * CHUNK - WINDOW + jnp.arange(kb_len)[None, :]
    valid = key_pos >= 0
    scores = jnp.where(valid[None, :, None, None, None, :], scores + alibi_bias(rel), NEG_INF)
    sk = sinks.astype(jnp.float32).reshape(N_KV_HEADS, GROUP, 1)
    out = sink_softmax_apply(scores, sk, vb, 'bckgqs,bcskd->bcqkgd')
    return out.reshape(B, S, ATTN_WIDTH)


def swa_sample(q, k_new, v_new, cache_k, cache_v, sinks):
    B, T = q.shape[0], q.shape[1]
    kk = jnp.concatenate([cache_k, k_new.astype(cache_k.dtype)], axis=1)
    vv = jnp.concatenate([cache_v, v_new.astype(cache_v.dtype)], axis=1)
    qg = q.reshape(B, T, N_KV_HEADS, GROUP, HEAD_DIM)
    scores = jnp.einsum('btkgd,bskd->bkgts', qg, kk).astype(jnp.float32) * (HEAD_DIM ** -0.5)
    rel = WINDOW + jnp.arange(T)[:, None] - jnp.arange(WINDOW + T)[None, :]
    scores = scores + alibi_bias(rel)
    sk = sinks.astype(jnp.float32).reshape(N_KV_HEADS, GROUP, 1)
    out = sink_softmax_apply(scores, sk, vv, 'bkgts,bskd->btkgd')
    return out.reshape(B, T, ATTN_WIDTH), kk[:, -WINDOW:], vv[:, -WINDOW:]


def pool_mixer(p_ext, pos, w_pool, pool_scale):
    B, L, C = p_ext.shape
    T = L - POOL_HIST
    f = p_ext.astype(jnp.float32)
    cs = jnp.cumsum(f, axis=1)
    cs_pad = jnp.pad(cs, ((0, 0), (POOL_MAXW, 0), (0, 0)))
    means = []
    for g, w in enumerate(POOL_WINDOWS):
        lo, hi = g * POOL_GROUP_DIM, (g + 1) * POOL_GROUP_DIM
        win = cs[:, POOL_HIST:, lo:hi] - cs_pad[:, POOL_MAXW - w + POOL_HIST:POOL_MAXW - w + L, lo:hi]
        cnt = jnp.minimum(pos + 1, w).astype(jnp.float32)[None, :, None]
        means.append(win / cnt)
    d = (jnp.concatenate(means, axis=-1) - f[:, POOL_HIST:]).reshape(B, T, N_POOL_GROUPS, POOL_GROUP_DIM)
    y = jnp.einsum('btgc,gcd->btgd', d, w_pool.astype(jnp.float32)).reshape(B, T, C)
    y = y * pool_scale.astype(jnp.float32)
    return y.astype(p_ext.dtype)


def conv_ffn(xn, conv_hist, w_up, conv_w, conv_b, w_down):
    h = xn @ w_up
    T = h.shape[1]
    h_ext = jnp.concatenate([conv_hist.astype(h.dtype), h], axis=1)
    hc = conv_b
    for j in range(CONV_WIDTH):
        hc = hc + conv_w[j] * h_ext[:, j:j + T]
    gate, val = jnp.split(hc, 2, axis=-1)
    y = (jax.nn.gelu(gate, approximate=False) * val) @ w_down
    return y, h_ext[:, -CONV_HIST:]


def trunk_layer(x, pos, kv_cache, pool_hist, conv_hist, lp):
    B, T = x.shape[0], x.shape[1]
    xn = rms_norm(x, lp['norm_mix'])
    proj = xn @ lp['w_in']
    cuts = [ATTN_WIDTH, ATTN_WIDTH + KV_WIDTH, ATTN_WIDTH + 2 * KV_WIDTH, ATTN_WIDTH + 2 * KV_WIDTH + POOL_WIDTH]
    q, k, v, pin, gates = jnp.split(proj, cuts, axis=-1)
    q = rms_norm(q.reshape(B, T, N_Q_HEADS, HEAD_DIM), lp['q_norm'])
    k = rms_norm(k.reshape(B, T, N_KV_HEADS, HEAD_DIM), lp['k_norm'])
    v = v.reshape(B, T, N_KV_HEADS, HEAD_DIM)
    if kv_cache is None:
        a = swa_prompt(q, k, v, lp['sinks'])
        k_state, v_state = k[:, -WINDOW:], v[:, -WINDOW:]
    else:
        a, k_state, v_state = swa_sample(q, k, v, kv_cache[0], kv_cache[1], lp['sinks'])
    p_ext = jnp.concatenate([pool_hist.astype(pin.dtype), pin], axis=1)
    pl = pool_mixer(p_ext, pos, lp['w_pool'], lp['pool_scale'])
    pool_state = p_ext[:, -POOL_HIST:]
    ya = a @ lp['w_br_attn']
    yb = pl @ lp['w_br_pool']
    g = jax.nn.sigmoid(gates.reshape(B, T, N_BRANCHES, D_MODEL) + lp['gate_bias'])
    mix = (g[:, :, 0] * ya + g[:, :, 1] * yb) @ lp['w_out']
    x = x + mix
    y, conv_state = conv_ffn(rms_norm(x, lp['norm_ffn']), conv_hist, lp['w_up'], lp['conv_w'], lp['conv_b'], lp['w_down'])
    x = x + y
    return x, k_state, v_state, pool_state, conv_state


def setup_inputs(seed: int = 0) -> dict:
    key = jax.random.key(seed)
    ks = jax.random.split(key, 24)

    def nrm(k, shape, scale):
        return jax.random.normal(k, shape, jnp.float32) * scale

    return {
        'x_prompt': nrm(ks[0], (BATCH, SEQ, D_MODEL), 1.0),
        'x_sample': nrm(ks[1], (DEC_BATCH, DEC_SEQ, D_MODEL), 1.0),
        'cache_k': nrm(ks[2], (DEPTH, DEC_BATCH, WINDOW, N_KV_HEADS, HEAD_DIM), 1.0),
        'cache_v': nrm(ks[3], (DEPTH, DEC_BATCH, WINDOW, N_KV_HEADS, HEAD_DIM), 1.0),
        'state_pool': nrm(ks[4], (DEPTH, DEC_BATCH, POOL_HIST, POOL_WIDTH), 1.0),
        'state_conv': nrm(ks[5], (DEPTH, DEC_BATCH, CONV_HIST, 2 * D_FF), 1.0),
        'norm_mix': 1.0 + nrm(ks[6], (DEPTH, D_MODEL), 0.05),
        'w_in': nrm(ks[7], (DEPTH, D_MODEL, IN_WIDTH), D_MODEL ** -0.5),
        'q_norm': 1.0 + nrm(ks[8], (DEPTH, HEAD_DIM), 0.05),
        'k_norm': 1.0 + nrm(ks[9], (DEPTH, HEAD_DIM), 0.05),
        'sinks': nrm(ks[10], (DEPTH, N_Q_HEADS), 1.0),
        'w_pool': nrm(ks[11], (DEPTH, N_POOL_GROUPS, POOL_GROUP_DIM, POOL_GROUP_DIM), POOL_GROUP_DIM ** -0.5),
        'pool_scale': 1.0 + nrm(ks[12], (DEPTH, POOL_WIDTH), 0.05),
        'w_br_attn': nrm(ks[13], (DEPTH, ATTN_WIDTH, D_MODEL), ATTN_WIDTH ** -0.5),
        'w_br_pool': nrm(ks[14], (DEPTH, POOL_WIDTH, D_MODEL), POOL_WIDTH ** -0.5),
        'gate_bias': nrm(ks[15], (DEPTH, N_BRANCHES, D_MODEL), 0.01),
        'w_out': nrm(ks[16], (DEPTH, D_MODEL, D_MODEL), D_MODEL ** -0.5),
        'norm_ffn': 1.0 + nrm(ks[17], (DEPTH, D_MODEL), 0.05),
        'w_up': nrm(ks[18], (DEPTH, D_MODEL, 2 * D_FF), D_MODEL ** -0.5),
        'conv_w': nrm(ks[19], (DEPTH, CONV_WIDTH, 2 * D_FF), CONV_WIDTH ** -0.5),
        'conv_b': nrm(ks[20], (DEPTH, 2 * D_FF), 0.01),
        'w_down': nrm(ks[21], (DEPTH, D_FF, D_MODEL), D_FF ** -0.5),
    }


def reference(x_prompt, x_sample, cache_k, cache_v, state_pool, state_conv,
              norm_mix, w_in, q_norm, k_norm, sinks, w_pool, pool_scale,
              w_br_attn, w_br_pool, gate_bias, w_out, norm_ffn, w_up, conv_w, conv_b, w_down):
    B, S = x_prompt.shape[0], x_prompt.shape[1]
    DB, T = x_sample.shape[0], x_sample.shape[1]
    pos_p = jnp.arange(S)
    pos_s = PAST_LEN + jnp.arange(T)
    zero_pool = jnp.zeros((B, POOL_HIST, POOL_WIDTH), x_prompt.dtype)
    zero_conv = jnp.zeros((B, CONV_HIST, 2 * D_FF), x_prompt.dtype)
    xp, xs = x_prompt, x_sample
    kp_l, vp_l, pp_l, cp_l = [], [], [], []
    ks_l, vs_l, ps_l, cs_l = [], [], [], []
    for l in range(DEPTH):
        lp = {
            'norm_mix': norm_mix[l], 'w_in': w_in[l], 'q_norm': q_norm[l], 'k_norm': k_norm[l],
            'sinks': sinks[l], 'w_pool': w_pool[l], 'pool_scale': pool_scale[l],
            'w_br_attn': w_br_attn[l], 'w_br_pool': w_br_pool[l], 'gate_bias': gate_bias[l],
            'w_out': w_out[l], 'norm_ffn': norm_ffn[l], 'w_up': w_up[l], 'conv_w': conv_w[l],
            'conv_b': conv_b[l], 'w_down': w_down[l],
        }
        xp, kp, vp, pp, cp = trunk_layer(xp, pos_p, None, zero_pool, zero_conv, lp)
        xs, ks_, vs_, ps_, cs_ = trunk_layer(xs, pos_s, (cache_k[l], cache_v[l]), state_pool[l], state_conv[l], lp)
        kp_l.append(kp); vp_l.append(vp); pp_l.append(pp); cp_l.append(cp)
        ks_l.append(ks_); vs_l.append(vs_); ps_l.append(ps_); cs_l.append(cs_)
    k_prompt = jnp.stack(kp_l, axis=0)
    v_prompt = jnp.stack(vp_l, axis=0)
    pool_prompt = jnp.stack(pp_l, axis=0)
    conv_prompt = jnp.stack(cp_l, axis=0)
    k_sample = jnp.stack(ks_l, axis=0)
    v_sample = jnp.stack(vs_l, axis=0)
    pool_sample = jnp.stack(ps_l, axis=0)
    conv_sample = jnp.stack(cs_l, axis=0)
    return (xp, xs, k_prompt, v_prompt, pool_prompt, conv_prompt, k_sample, v_sample, pool_sample, conv_sample)
```

```python
import functools
import math
from typing import NamedTuple

import jax
import jax.numpy as jnp
from jax import lax
from jax.experimental import pallas as pl
from jax.experimental.pallas import tpu as pltpu

D_MODEL = 1024
CHUNK = 64
WINDOW = 128
HEAD_DIM = 64
N_Q_HEADS = 8
N_KV_HEADS = 2
GROUP = N_Q_HEADS // N_KV_HEADS
ATTN_WIDTH = N_Q_HEADS * HEAD_DIM
KV_WIDTH = N_KV_HEADS * HEAD_DIM
POOL_WINDOWS = (2, 4, 8, 16)
POOL_WIDTH = 512
POOL_GROUP_DIM = POOL_WIDTH // len(POOL_WINDOWS)
POOL_HIST = max(POOL_WINDOWS) - 1
QKVP_WIDTH = ATTN_WIDTH + 2 * KV_WIDTH + POOL_WIDTH
D_FF = 2816
CONV_WIDTH = 3
CONV_HIST = CONV_WIDTH - 1
PAST_LEN = 1024
EPS = 1e-6
NEG_INF = -1e30

LANES = 128
SUBLANES = 8
KEY_BLOCK = 256
POOL_HALO = 16
CONV_HALO = SUBLANES
FF_CHUNK = 256
PROMPT_TILE = 256
VMEM_LIMIT_BYTES = 56 * 1024 * 1024

F32 = jnp.float32
BF16 = jnp.bfloat16


class TileCfg(NamedTuple):
    n_seq: int
    rows: int
    n_tiles: int
    pos0: int

    @property
    def q_chunk(self):
        return min(CHUNK, self.rows)

    @property
    def kv_halo(self):
        return KEY_BLOCK - self.q_chunk

    @property
    def tile_tokens(self):
        return self.n_seq * self.rows


def _rms_norm(x, gain):
    ms = jnp.mean(x * x, axis=-1, keepdims=True)
    return x * lax.rsqrt(ms + EPS) * gain


def _half_lane_mask(shape):
    return lax.broadcasted_iota(jnp.int32, shape, len(shape) - 1) < HEAD_DIM


def _head_pair_rms_norm(x, gain):
    low = _half_lane_mask(x.shape)
    sq = x * x
    ms_lo = jnp.sum(jnp.where(low, sq, 0.0), axis=-1, keepdims=True) * (1.0 / HEAD_DIM)
    ms_hi = jnp.sum(jnp.where(low, 0.0, sq), axis=-1, keepdims=True) * (1.0 / HEAD_DIM)
    inv = jnp.where(low, lax.rsqrt(ms_lo + EPS), lax.rsqrt(ms_hi + EPS))
    return x * inv * gain


def _dup_halves(x):
    low = _half_lane_mask(x.shape)
    swapped = pltpu.roll(x, HEAD_DIM, axis=x.ndim - 1)
    return jnp.where(low, x, swapped), jnp.where(low, swapped, x)


def _mixer_kernel(cfg, sinks_ref, x_ref, kh_ref, vh_ref, ph_ref, nmix_ref, win_ref, qg_ref, kg_ref,
                  wpool_ref, pscale_ref, wba_ref, wbp_ref, gbias_ref, wout_ref,
                  y_ref, kst_ref, vst_ref, pst_ref,
                  q_s, k2_s, v2_s, p_s, a_s, bias_s, sink_s):
    ns, rows, qc, halo = cfg.n_seq, cfg.rows, cfg.q_chunk, cfg.kv_halo
    tm = cfg.tile_tokens
    tile = pl.program_id(1)
    first_call = jnp.logical_and(pl.program_id(0) == 0, tile == 0)

    @pl.when(first_call)
    def _build_tables():
        shape = (GROUP * qc, KEY_BLOCK)
        r = lax.broadcasted_iota(jnp.int32, shape, 0)
        s = lax.broadcasted_iota(jnp.int32, shape, 1)
        head_in_group = r // qc
        rel = (halo + r % qc) - s
        dist = jnp.abs(rel).astype(F32)
        in_window = s >= halo - WINDOW
        for kh in range(N_KV_HEADS):
            slope = jnp.zeros(shape, F32)
            for g in range(GROUP):
                h = kh * GROUP + g
                slope = jnp.where(head_in_group == g, 2.0 ** (-8.0 * (h + 1) / N_Q_HEADS), slope)
            bias_s[kh] = jnp.where(in_window, -slope * dist, NEG_INF)
            col = jnp.zeros((GROUP * qc, LANES), F32)
            rr = lax.broadcasted_iota(jnp.int32, (GROUP * qc, LANES), 0) // qc
            for g in range(GROUP):
                col = jnp.where(rr == g, sinks_ref[kh * GROUP + g], col)
            sink_s[kh] = col

    @pl.when(tile == 0)
    def _load_history():
        zeros = jnp.zeros((ns, halo - WINDOW, LANES), BF16)
        for src, dst in ((kh_ref, k2_s), (vh_ref, v2_s)):
            d0, d1 = _dup_halves(src[...].reshape(ns * WINDOW, LANES))
            for kh, d in enumerate((d0, d1)):
                dst[kh, :, 0:halo - WINDOW, :] = zeros
                dst[kh, :, halo - WINDOW:halo, :] = d.astype(BF16).reshape(ns, WINDOW, LANES)
        p_s[:, 0:POOL_HALO, :] = ph_ref[...]

    x = x_ref[...].reshape(tm, D_MODEL)
    xnb = _rms_norm(x, nmix_ref[...]).astype(BF16)
    qkvp = jnp.dot(xnb, win_ref[:, 0:QKVP_WIDTH], preferred_element_type=F32)

    low_t = _half_lane_mask((tm, LANES))
    for j in range(ATTN_WIDTH // LANES):
        qn = _head_pair_rms_norm(qkvp[:, j * LANES:(j + 1) * LANES], qg_ref[...])
        q_s[0, :, j * LANES:(j + 1) * LANES] = jnp.where(low_t, qn, 0.0).astype(BF16)
        q_s[1, :, j * LANES:(j + 1) * LANES] = jnp.where(low_t, 0.0, qn).astype(BF16)
    kn = _head_pair_rms_norm(qkvp[:, ATTN_WIDTH:ATTN_WIDTH + KV_WIDTH], kg_ref[...])
    vv = qkvp[:, ATTN_WIDTH + KV_WIDTH:ATTN_WIDTH + 2 * KV_WIDTH]
    pin = qkvp[:, ATTN_WIDTH + 2 * KV_WIDTH:QKVP_WIDTH]
    for val, dst in ((kn, k2_s), (vv, v2_s)):
        d0, d1 = _dup_halves(val)
        dst[0, :, halo:halo + rows, :] = d0.astype(BF16).reshape(ns, rows, LANES)
        dst[1, :, halo:halo + rows, :] = d1.astype(BF16).reshape(ns, rows, LANES)
    p_s[:, POOL_HALO:POOL_HALO + rows, :] = pin.reshape(ns, rows, POOL_WIDTH)

    if rows >= WINDOW:
        @pl.when(tile == cfg.n_tiles - 1)
        def _write_kv_state():
            kst_ref[...] = kn.reshape(ns, rows, LANES)[:, rows - WINDOW:, :]
            vst_ref[...] = vv.reshape(ns, rows, LANES)[:, rows - WINDOW:, :]
    else:
        kst_ref[:, 0:WINDOW - rows, :] = kh_ref[:, rows:, :]
        vst_ref[:, 0:WINDOW - rows, :] = vh_ref[:, rows:, :]
        kst_ref[:, WINDOW - rows:, :] = kn.reshape(ns, rows, LANES)
        vst_ref[:, WINDOW - rows:, :] = vv.reshape(ns, rows, LANES)

    def attend(seq, row0):
        low_q = _half_lane_mask((qc, LANES))
        for kh in range(N_KV_HEADS):
            qstack = jnp.concatenate(
                [q_s[half, pl.ds(seq * rows + row0, qc), j * LANES:(j + 1) * LANES]
                 for j in (2 * kh, 2 * kh + 1) for half in range(2)], axis=0)
            kc = k2_s[kh, seq, pl.ds(row0, KEY_BLOCK), :]
            sc = lax.dot_general(qstack, kc, (((1,), (1,)), ((), ())), preferred_element_type=F32)
            sc = sc + bias_s[kh]
            if cfg.pos0 < WINDOW:
                s_idx = lax.broadcasted_iota(jnp.int32, sc.shape, 1)
                key_pos = s_idx + (cfg.pos0 + tile * rows + row0 - halo)
                sc = jnp.where(key_pos >= 0, sc, NEG_INF)
            sink = sink_s[kh][:, 0:1]
            m = jnp.maximum(jnp.max(sc, axis=-1, keepdims=True), sink)
            p = jnp.exp(sc - m)
            denom = jnp.sum(p, axis=-1, keepdims=True) + jnp.exp(sink - m)
            vc = v2_s[kh, seq, pl.ds(row0, KEY_BLOCK), :]
            o = jnp.dot(p.astype(BF16), vc, preferred_element_type=F32) / denom
            for jj in range(2):
                pair = jnp.where(low_q, o[(2 * jj) * qc:(2 * jj + 1) * qc],
                                 o[(2 * jj + 1) * qc:(2 * jj + 2) * qc])
                col = (2 * kh + jj) * LANES
                a_s[pl.ds(seq * rows + row0, qc), col:col + LANES] = pair.astype(BF16)

    n_chunks = rows // qc
    for seq in range(ns):
        if n_chunks == 1:
            attend(seq, 0)
        else:
            def body(c, carry, seq=seq):
                attend(seq, pl.multiple_of(c * qc, qc))
                return carry
            lax.fori_loop(0, n_chunks, body, 0)

    row_pos = cfg.pos0 + tile * rows + lax.broadcasted_iota(jnp.int32, (rows, 1), 0)
    pooled = []
    for g, w in enumerate(POOL_WINDOWS):
        cols = slice(g * POOL_GROUP_DIM, (g + 1) * POOL_GROUP_DIM)
        win = p_s[:, POOL_HALO:POOL_HALO + rows, cols]
        for j in range(1, w):
            win = win + p_s[:, POOL_HALO - j:POOL_HALO - j + rows, cols]
        if cfg.pos0 >= POOL_HIST:
            mean = win * (1.0 / w)
        else:
            mean = win * (1.0 / jnp.minimum(row_pos + 1, w).astype(F32))
        d = (mean - p_s[:, POOL_HALO:POOL_HALO + rows, cols]).reshape(tm, POOL_GROUP_DIM)
        yg = jnp.dot(d.astype(BF16), wpool_ref[g], preferred_element_type=F32)
        pooled.append((yg * pscale_ref[:, cols]).astype(BF16))
    plb = jnp.concatenate(pooled, axis=-1)

    @pl.when(tile == cfg.n_tiles - 1)
    def _write_pool_state():
        pst_ref[...] = p_s[:, rows:rows + POOL_HALO, :]

    ya = jnp.dot(a_s[...], wba_ref[...], preferred_element_type=F32)
    g0 = jax.nn.sigmoid(jnp.dot(xnb, win_ref[:, QKVP_WIDTH:QKVP_WIDTH + D_MODEL],
                                preferred_element_type=F32) + gbias_ref[0:1, :])
    merged = g0 * ya
    yb = jnp.dot(plb, wbp_ref[...], preferred_element_type=F32)
    g1 = jax.nn.sigmoid(jnp.dot(xnb, win_ref[:, QKVP_WIDTH + D_MODEL:QKVP_WIDTH + 2 * D_MODEL],
                                preferred_element_type=F32) + gbias_ref[1:2, :])
    merged = merged + g1 * yb
    mix = jnp.dot(merged.astype(BF16), wout_ref[...], preferred_element_type=F32)
    y_ref[...] = (x + mix).reshape(ns, rows, D_MODEL)

    if cfg.n_tiles > 1:
        for buf in (k2_s, v2_s):
            buf[:, :, 0:halo, :] = buf[:, :, rows:rows + halo, :]
        p_s[:, 0:POOL_HALO, :] = p_s[:, rows:rows + POOL_HALO, :]


def _ffn_kernel(cfg, x_ref, ch_ref, nffn_ref, wup_ref, cw_ref, cb_ref, wdown_ref,
                y_ref, cst_ref, h_s, halo_s, act_s):
    ns, rows, tm = cfg.n_seq, cfg.rows, cfg.tile_tokens
    tile = pl.program_id(1)

    @pl.when(tile == 0)
    def _load_history():
        halo_s[...] = ch_ref[...]

    x = x_ref[...].reshape(tm, D_MODEL)
    xnb = _rms_norm(x, nffn_ref[...]).astype(BF16)
    sqrt_half = math.sqrt(0.5)
    for c in range(D_FF // FF_CHUNK):
        conv = []
        for part in range(2):
            cols = slice(part * D_FF + c * FF_CHUNK, part * D_FF + (c + 1) * FF_CHUNK)
            h = jnp.dot(xnb, wup_ref[:, cols], preferred_element_type=F32)
            h_s[part, :, 0:CONV_HALO, :] = halo_s[:, :, cols]
            h_s[part, :, CONV_HALO:CONV_HALO + rows, :] = h.reshape(ns, rows, FF_CHUNK)
            hc = cb_ref[:, cols]
            for j in range(CONV_WIDTH):
                off = CONV_HALO - CONV_HIST + j
                hc = hc + cw_ref[j:j + 1, cols] * h_s[part, :, off:off + rows, :]
            halo_s[:, :, cols] = h_s[part, :, rows:rows + CONV_HALO, :]
            conv.append(hc.reshape(tm, FF_CHUNK))
        gate, val = conv
        gelu = 0.5 * gate * (1.0 + lax.erf(gate * sqrt_half))
        act_s[:, c * FF_CHUNK:(c + 1) * FF_CHUNK] = (gelu * val).astype(BF16)
    y = jnp.dot(act_s[...], wdown_ref[...], preferred_element_type=F32)
    y_ref[...] = (x + y).reshape(ns, rows, D_MODEL)

    @pl.when(tile == cfg.n_tiles - 1)
    def _write_conv_state():
        cst_ref[...] = halo_s[...]


def _const_spec(shape):
    return pl.BlockSpec(shape, lambda b, i: (0,) * len(shape))


def _seq_spec(ns, rows, width):
    return pl.BlockSpec((ns, rows, width), lambda b, i: (b, 0, 0))


def _compiler_params():
    return pltpu.CompilerParams(dimension_semantics=("arbitrary", "arbitrary"),
                                vmem_limit_bytes=VMEM_LIMIT_BYTES)


def _mixer_call(cfg, n_groups, x, k_hist, v_hist, p_hist, lp):
    ns, rows, qc, halo, tm = cfg.n_seq, cfg.rows, cfg.q_chunk, cfg.kv_halo, cfg.tile_tokens
    n_seq_total = n_groups * ns
    tile_spec = pl.BlockSpec((ns, rows, D_MODEL), lambda b, i: (b, i, 0))
    in_specs = [
        pl.BlockSpec(memory_space=pltpu.SMEM),
        tile_spec,
        _seq_spec(ns, WINDOW, KV_WIDTH), _seq_spec(ns, WINDOW, KV_WIDTH), _seq_spec(ns, POOL_HALO, POOL_WIDTH),
        _const_spec((1, D_MODEL)), _const_spec(lp['w_in'].shape),
        _const_spec((1, LANES)), _const_spec((1, LANES)),
        _const_spec(lp['w_pool'].shape), _const_spec((1, POOL_WIDTH)),
        _const_spec(lp['w_br_attn'].shape), _const_spec(lp['w_br_pool'].shape),
        _const_spec(lp['gate_bias'].shape), _const_spec(lp['w_out'].shape),
    ]
    out_specs = [tile_spec, _seq_spec(ns, WINDOW, KV_WIDTH), _seq_spec(ns, WINDOW, KV_WIDTH),
                 _seq_spec(ns, POOL_HALO, POOL_WIDTH)]
    out_shape = [jax.ShapeDtypeStruct(x.shape, F32),
                 jax.ShapeDtypeStruct((n_seq_total, WINDOW, KV_WIDTH), F32),
                 jax.ShapeDtypeStruct((n_seq_total, WINDOW, KV_WIDTH), F32),
                 jax.ShapeDtypeStruct((n_seq_total, POOL_HALO, POOL_WIDTH), F32)]
    scratch = [
        pltpu.VMEM((2, tm, ATTN_WIDTH), BF16),
        pltpu.VMEM((N_KV_HEADS, ns, halo + rows, LANES), BF16),
        pltpu.VMEM((N_KV_HEADS, ns, halo + rows, LANES), BF16),
        pltpu.VMEM((ns, POOL_HALO + rows, POOL_WIDTH), F32),
        pltpu.VMEM((tm, ATTN_WIDTH), BF16),
        pltpu.VMEM((N_KV_HEADS, GROUP * qc, KEY_BLOCK), F32),
        pltpu.VMEM((N_KV_HEADS, GROUP * qc, LANES), F32),
    ]
    return pl.pallas_call(
        functools.partial(_mixer_kernel, cfg),
        grid=(n_groups, cfg.n_tiles),
        in_specs=in_specs, out_specs=out_specs, out_shape=out_shape, scratch_shapes=scratch,
        compiler_params=_compiler_params(),
        name=f"mixer_r{rows}",
    )(lp['sinks'], x, k_hist, v_hist, p_hist, lp['norm_mix'], lp['w_in'], lp['q_gain'], lp['k_gain'],
      lp['w_pool'], lp['pool_scale'], lp['w_br_attn'], lp['w_br_pool'], lp['gate_bias'], lp['w_out'])


def _ffn_call(cfg, n_groups, x, c_hist, lp):
    ns, rows, tm = cfg.n_seq, cfg.rows, cfg.tile_tokens
    n_seq_total = n_groups * ns
    tile_spec = pl.BlockSpec((ns, rows, D_MODEL), lambda b, i: (b, i, 0))
    in_specs = [
        tile_spec, _seq_spec(ns, CONV_HALO, 2 * D_FF),
        _const_spec((1, D_MODEL)), _const_spec(lp['w_up'].shape),
        _const_spec(lp['conv_w'].shape), _const_spec((1, 2 * D_FF)), _const_spec(lp['w_down'].shape),
    ]
    out_specs = [tile_spec, _seq_spec(ns, CONV_HALO, 2 * D_FF)]
    out_shape = [jax.ShapeDtypeStruct(x.shape, F32),
                 jax.ShapeDtypeStruct((n_seq_total, CONV_HALO, 2 * D_FF), F32)]
    scratch = [
        pltpu.VMEM((2, ns, CONV_HALO + rows, FF_CHUNK), F32),
        pltpu.VMEM((ns, CONV_HALO, 2 * D_FF), F32),
        pltpu.VMEM((tm, D_FF), BF16),
    ]
    return pl.pallas_call(
        functools.partial(_ffn_kernel, cfg),
        grid=(n_groups, cfg.n_tiles),
        in_specs=in_specs, out_specs=out_specs, out_shape=out_shape, scratch_shapes=scratch,
        compiler_params=_compiler_params(),
        name=f"ffn_r{rows}",
    )(x, c_hist, lp['norm_ffn'], lp['w_up'], lp['conv_w'], lp['conv_b'], lp['w_down'])


def _layer(cfg, n_groups, x, k_hist, v_hist, p_hist, c_hist, lp):
    p_pad = jnp.pad(p_hist, ((0, 0), (POOL_HALO - POOL_HIST, 0), (0, 0)))
    c_pad = jnp.pad(c_hist, ((0, 0), (CONV_HALO - CONV_HIST, 0), (0, 0)))
    x, k_state, v_state, p_state = _mixer_call(cfg, n_groups, x, k_hist, v_hist, p_pad, lp)
    x, c_state = _ffn_call(cfg, n_groups, x, c_pad, lp)
    return x, k_state, v_state, p_state[:, POOL_HALO - POOL_HIST:], c_state[:, CONV_HALO - CONV_HIST:]


def kernel(x_prompt, x_sample, cache_k, cache_v, state_pool, state_conv, norm_mix, w_in, q_norm, k_norm,
           sinks, w_pool, pool_scale, w_br_attn, w_br_pool, gate_bias, w_out, norm_ffn, w_up, conv_w,
           conv_b, w_down):
    depth = w_in.shape[0]
    batch, seq = x_prompt.shape[0], x_prompt.shape[1]
    dec_batch, dec_seq = x_sample.shape[0], x_sample.shape[1]
    assert seq % PROMPT_TILE == 0 and dec_seq % SUBLANES == 0 and dec_seq <= CHUNK
    prompt_cfg = TileCfg(n_seq=1, rows=PROMPT_TILE, n_tiles=seq // PROMPT_TILE, pos0=0)
    sample_cfg = TileCfg(n_seq=dec_batch, rows=dec_seq, n_tiles=1, pos0=PAST_LEN)

    score_scale = HEAD_DIM ** -0.5
    zeros_kv = jnp.zeros((batch, WINDOW, KV_WIDTH), F32)
    zeros_pool = jnp.zeros((batch, POOL_HIST, POOL_WIDTH), F32)
    zeros_conv = jnp.zeros((batch, CONV_HIST, 2 * D_FF), F32)

    xp, xs = x_prompt, x_sample
    states_p, states_s = [], []
    for l in range(depth):
        lp = {
            'sinks': sinks[l],
            'norm_mix': norm_mix[l][None, :], 'w_in': w_in[l].astype(BF16),
            'q_gain': (jnp.tile(q_norm[l], 2) * score_scale)[None, :], 'k_gain': jnp.tile(k_norm[l], 2)[None, :],
            'w_pool': w_pool[l].astype(BF16), 'pool_scale': pool_scale[l][None, :],
            'w_br_attn': w_br_attn[l].astype(BF16), 'w_br_pool': w_br_pool[l].astype(BF16),
            'gate_bias': gate_bias[l], 'w_out': w_out[l].astype(BF16),
            'norm_ffn': norm_ffn[l][None, :], 'w_up': w_up[l].astype(BF16), 'conv_w': conv_w[l],
            'conv_b': conv_b[l][None, :], 'w_down': w_down[l].astype(BF16),
        }
        xp, *st_p = _layer(prompt_cfg, batch, xp, zeros_kv, zeros_kv, zeros_pool, zeros_conv, lp)
        xs, *st_s = _layer(sample_cfg, 1, xs,
                           cache_k[l].reshape(dec_batch, WINDOW, KV_WIDTH),
                           cache_v[l].reshape(dec_batch, WINDOW, KV_WIDTH),
                           state_pool[l], state_conv[l], lp)
        states_p.append(st_p)
        states_s.append(st_s)

    def stacked(states, idx, n_seq):
        arr = jnp.stack([st[idx] for st in states], axis=0)
        if idx < 2:
            arr = arr.reshape(depth, n_seq, WINDOW, N_KV_HEADS, HEAD_DIM)
        return arr

    return (xp, xs,
            stacked(states_p, 0, batch), stacked(states_p, 1, batch),
            stacked(states_p, 2, batch), stacked(states_p, 3, batch),
            stacked(states_s, 0, dec_batch), stacked(states_s, 1, dec_batch),
            stacked(states_s, 2, dec_batch), stacked(states_s, 3, dec_batch))
```

```python
import functools
import math
from typing import NamedTuple

import jax
import jax.numpy as jnp
from jax import lax
from jax.experimental import pallas as pl
from jax.experimental.pallas import tpu as pltpu

D_MODEL = 1024
CHUNK = 64
WINDOW = 128
HEAD_DIM = 64
N_Q_HEADS = 8
N_KV_HEADS = 2
GROUP = N_Q_HEADS // N_KV_HEADS
ATTN_WIDTH = N_Q_HEADS * HEAD_DIM
KV_WIDTH = N_KV_HEADS * HEAD_DIM
POOL_WINDOWS = (2, 4, 8, 16)
POOL_WIDTH = 512
POOL_GROUP_DIM = POOL_WIDTH // len(POOL_WINDOWS)
POOL_HIST = max(POOL_WINDOWS) - 1
QKVP_WIDTH = ATTN_WIDTH + 2 * KV_WIDTH + POOL_WIDTH
D_FF = 2816
CONV_WIDTH = 3
CONV_HIST = CONV_WIDTH - 1
PAST_LEN = 1024
EPS = 1e-6
NEG_INF = -1e30

LANES = 128
SUBLANES = 8
KEY_BLOCK = 256
QUERY_BLOCK = 128
POOL_HALO = 16
CONV_HALO = SUBLANES
FF_CHUNK = 256
FF_LOOKAHEAD = 3
PROMPT_TILE = 256
VMEM_LIMIT_BYTES = 56 * 1024 * 1024

F32 = jnp.float32
BF16 = jnp.bfloat16


class TileCfg(NamedTuple):
    n_seq: int
    rows: int
    n_tiles: int
    pos0: int

    @property
    def q_block(self):
        return min(QUERY_BLOCK, self.rows)

    @property
    def keys_on_rows(self):
        return self.rows >= QUERY_BLOCK

    @property
    def kv_halo(self):
        return KEY_BLOCK - self.q_block

    @property
    def tile_tokens(self):
        return self.n_seq * self.rows

    @property
    def n_early_blocks(self):
        return max(0, -(-(WINDOW - self.pos0) // self.q_block))


def _rms_norm(x, gain):
    ms = jnp.mean(x * x, axis=-1, keepdims=True)
    return x * lax.rsqrt(ms + EPS) * gain


def _half_lane_mask(shape):
    return lax.broadcasted_iota(jnp.int32, shape, len(shape) - 1) < HEAD_DIM


def _head_pair_rms_norm(x, gain):
    low = _half_lane_mask(x.shape)
    sq = x * x
    ms_lo = jnp.sum(jnp.where(low, sq, 0.0), axis=-1, keepdims=True) * (1.0 / HEAD_DIM)
    ms_hi = jnp.sum(jnp.where(low, 0.0, sq), axis=-1, keepdims=True) * (1.0 / HEAD_DIM)
    inv = jnp.where(low, lax.rsqrt(ms_lo + EPS), lax.rsqrt(ms_hi + EPS))
    return x * inv * gain


def _dup_halves(x):
    low = _half_lane_mask(x.shape)
    swapped = pltpu.roll(x, HEAD_DIM, axis=x.ndim - 1)
    return jnp.where(low, x, swapped), jnp.where(low, swapped, x)


def _build_attention_tables(cfg, sinks_ref, bias_s, sink_s):
    qb, halo = cfg.q_block, cfg.kv_halo
    q_axis, k_axis = (1, 0) if cfg.keys_on_rows else (0, 1)
    shape = (KEY_BLOCK, GROUP * qb) if cfg.keys_on_rows else (GROUP * qb, KEY_BLOCK)
    stacked = lax.broadcasted_iota(jnp.int32, shape, q_axis)
    s = lax.broadcasted_iota(jnp.int32, shape, k_axis)
    head_in_group = stacked // qb
    t = stacked % qb
    dist = jnp.abs((halo + t) - s).astype(F32)
    chunk_start = halo + (t // CHUNK) * CHUNK
    in_window = jnp.logical_and(s >= chunk_start - WINDOW, s < chunk_start + CHUNK)
    sink_shape = (SUBLANES, GROUP * qb) if cfg.keys_on_rows else (GROUP * qb, LANES)
    sink_head = lax.broadcasted_iota(jnp.int32, sink_shape, q_axis) // qb
    for kh in range(N_KV_HEADS):
        slope = jnp.zeros(shape, F32)
        sink = jnp.zeros(sink_shape, F32)
        for g in range(GROUP):
            h = kh * GROUP + g
            slope = jnp.where(head_in_group == g, 2.0 ** (-8.0 * (h + 1) / N_Q_HEADS), slope)
            sink = jnp.where(sink_head == g, sinks_ref[h], sink)
        sink_s[kh] = sink
        for v in range(1 + cfg.n_early_blocks):
            visible = in_window if v == 0 else jnp.logical_and(in_window, s >= halo - cfg.pos0 - (v - 1) * qb)
            bias_s[v, kh] = jnp.where(visible, -slope * dist, NEG_INF)


def _mixer_kernel(cfg, sinks_ref, x_ref, kh_ref, vh_ref, ph_ref, nmix_ref, win_ref, qg_ref, kg_ref,
                  wpool_ref, pscale_ref, wba_ref, wbp_ref, gbias_ref, wout_ref,
                  y_ref, kst_ref, vst_ref, pst_ref,
                  q_s, k_s, v_s, p_s, a_s, g_s, bias_s, sink_s):
    ns, rows, qb, halo = cfg.n_seq, cfg.rows, cfg.q_block, cfg.kv_halo
    tm = cfg.tile_tokens
    tile = pl.program_id(1)

    @pl.when(jnp.logical_and(pl.program_id(0) == 0, tile == 0))
    def _first_call():
        _build_attention_tables(cfg, sinks_ref, bias_s, sink_s)

    def store_kv(k_val, v_val, row_lo, n):
        for kh, (kd, vd) in enumerate(zip(_dup_halves(k_val), _dup_halves(v_val))):
            k_s[kh, :, row_lo:row_lo + n, :] = kd.astype(BF16).reshape(ns, n, LANES)
            if cfg.keys_on_rows:
                for seq in range(ns):
                    v_s[kh, seq, :, row_lo:row_lo + n] = vd[seq * n:(seq + 1) * n].T.astype(BF16)
            else:
                v_s[kh, :, row_lo:row_lo + n, :] = vd.astype(BF16).reshape(ns, n, LANES)

    @pl.when(tile == 0)
    def _load_history():
        if halo > WINDOW:
            zeros = jnp.zeros((ns * (halo - WINDOW), LANES), F32)
            store_kv(zeros, zeros, 0, halo - WINDOW)
        store_kv(kh_ref[...].reshape(ns * WINDOW, LANES), vh_ref[...].reshape(ns * WINDOW, LANES),
                 halo - WINDOW, WINDOW)
        p_s[:, 0:POOL_HALO, :] = ph_ref[...]

    x = x_ref[...].reshape(tm, D_MODEL)
    xnb = _rms_norm(x, nmix_ref[...]).astype(BF16)
    qkvp = jnp.dot(xnb, win_ref[:, 0:QKVP_WIDTH], preferred_element_type=F32)

    def gate(idx):
        cols = slice(QKVP_WIDTH + idx * D_MODEL, QKVP_WIDTH + (idx + 1) * D_MODEL)
        g_s[idx] = jax.nn.sigmoid(jnp.dot(xnb, win_ref[:, cols], preferred_element_type=F32)
                                  + gbias_ref[idx:idx + 1, :])

    gate(0)

    low_t = _half_lane_mask((tm, LANES))
    for j in range(ATTN_WIDTH // LANES):
        qn = _head_pair_rms_norm(qkvp[:, j * LANES:(j + 1) * LANES], qg_ref[...])
        q_s[0, :, j * LANES:(j + 1) * LANES] = jnp.where(low_t, qn, 0.0).astype(BF16)
        q_s[1, :, j * LANES:(j + 1) * LANES] = jnp.where(low_t, 0.0, qn).astype(BF16)
    kn = _head_pair_rms_norm(qkvp[:, ATTN_WIDTH:ATTN_WIDTH + KV_WIDTH], kg_ref[...])
    vv = qkvp[:, ATTN_WIDTH + KV_WIDTH:ATTN_WIDTH + 2 * KV_WIDTH]
    pin = qkvp[:, ATTN_WIDTH + 2 * KV_WIDTH:QKVP_WIDTH]
    store_kv(kn, vv, halo, rows)
    p_s[:, POOL_HALO:POOL_HALO + rows, :] = pin.reshape(ns, rows, POOL_WIDTH)

    if rows >= WINDOW:
        kst_ref[...] = kn.reshape(ns, rows, LANES)[:, rows - WINDOW:, :]
        vst_ref[...] = vv.reshape(ns, rows, LANES)[:, rows - WINDOW:, :]
    else:
        kst_ref[:, 0:WINDOW - rows, :] = kh_ref[:, rows:, :]
        vst_ref[:, 0:WINDOW - rows, :] = vh_ref[:, rows:, :]
        kst_ref[:, WINDOW - rows:, :] = kn.reshape(ns, rows, LANES)
        vst_ref[:, WINDOW - rows:, :] = vv.reshape(ns, rows, LANES)

    red_axis = 0 if cfg.keys_on_rows else 1
    units = [(seq, block, kh) for seq in range(ns) for block in range(rows // qb) for kh in range(N_KV_HEADS)]

    def sink_of(kh):
        return sink_s[kh][0:1, :] if cfg.keys_on_rows else sink_s[kh][:, 0:1]

    def scores(seq, block, kh):
        row0, tok0 = block * qb, seq * rows + block * qb
        table = jnp.where(tile == 0, block + 1, 0) if block < cfg.n_early_blocks else 0
        qstack = jnp.concatenate(
            [q_s[half, tok0:tok0 + qb, j * LANES:(j + 1) * LANES]
             for j in (2 * kh, 2 * kh + 1) for half in range(2)], axis=0)
        kc = k_s[kh, seq, row0:row0 + KEY_BLOCK, :]
        lhs, rhs = (kc, qstack) if cfg.keys_on_rows else (qstack, kc)
        sc = lax.dot_general(lhs, rhs, (((1,), (1,)), ((), ())), preferred_element_type=F32)
        return sc + bias_s[table, kh]

    def weights(sc, kh):
        sink = sink_of(kh)
        m = jnp.maximum(jnp.max(sc, axis=red_axis, keepdims=True), sink)
        p = jnp.exp(sc - m)
        inv = 1.0 / (jnp.sum(p, axis=red_axis, keepdims=True) + jnp.exp(sink - m))
        return p.astype(BF16), inv

    def outputs(seq, block, kh, p, inv):
        row0, tok0 = block * qb, seq * rows + block * qb
        if cfg.keys_on_rows:
            o = jnp.dot(v_s[kh, seq, :, row0:row0 + KEY_BLOCK], p, preferred_element_type=F32) * inv
            low = lax.broadcasted_iota(jnp.int32, (LANES, qb), 0) < HEAD_DIM
            for jj in range(2):
                pair = jnp.where(low, o[:, (2 * jj) * qb:(2 * jj + 1) * qb],
                                 o[:, (2 * jj + 1) * qb:(2 * jj + 2) * qb])
                r0 = (2 * kh + jj) * LANES
                a_s[r0:r0 + LANES, tok0:tok0 + qb] = pair.astype(BF16)
        else:
            o = jnp.dot(p, v_s[kh, seq, row0:row0 + KEY_BLOCK, :], preferred_element_type=F32) * inv
            low = _half_lane_mask((qb, LANES))
            for jj in range(2):
                pair = jnp.where(low, o[(2 * jj) * qb:(2 * jj + 1) * qb],
                                 o[(2 * jj + 1) * qb:(2 * jj + 2) * qb])
                c0 = (2 * kh + jj) * LANES
                a_s[tok0:tok0 + qb, c0:c0 + LANES] = pair.astype(BF16)

    def pool_mixer():
        row_pos = cfg.pos0 + tile * rows + lax.broadcasted_iota(jnp.int32, (rows, 1), 0)
        pooled = []
        for g, w in enumerate(POOL_WINDOWS):
            cols = slice(g * POOL_GROUP_DIM, (g + 1) * POOL_GROUP_DIM)
            win = p_s[:, POOL_HALO:POOL_HALO + rows, cols]
            for j in range(1, w):
                win = win + p_s[:, POOL_HALO - j:POOL_HALO - j + rows, cols]
            if cfg.pos0 >= POOL_HIST:
                mean = win * (1.0 / w)
            else:
                mean = win * (1.0 / jnp.minimum(row_pos + 1, w).astype(F32))
            d = (mean - p_s[:, POOL_HALO:POOL_HALO + rows, cols]).reshape(tm, POOL_GROUP_DIM)
            yg = jnp.dot(d.astype(BF16), wpool_ref[g], preferred_element_type=F32)
            pooled.append((yg * pscale_ref[:, cols]).astype(BF16))
        return jnp.concatenate(pooled, axis=-1)

    all_scores = [scores(*u) for u in units]
    plb = pool_mixer()
    pst_ref[...] = p_s[:, rows:rows + POOL_HALO, :]
    gate(1)
    all_weights = [weights(sc, u[2]) for sc, u in zip(all_scores, units)]
    yb = jnp.dot(plb, wbp_ref[...], preferred_element_type=F32)
    for u, (p, inv) in zip(units, all_weights):
        outputs(*u, p, inv)

    if cfg.keys_on_rows:
        ya = lax.dot_general(a_s[...], wba_ref[...], (((0,), (0,)), ((), ())), preferred_element_type=F32)
    else:
        ya = jnp.dot(a_s[...], wba_ref[...], preferred_element_type=F32)
    merged = g_s[0] * ya + g_s[1] * yb
    mix = jnp.dot(merged.astype(BF16), wout_ref[...], preferred_element_type=F32)
    y_ref[...] = (x + mix).reshape(ns, rows, D_MODEL)

    if cfg.n_tiles > 1:
        k_s[:, :, 0:halo, :] = k_s[:, :, rows:rows + halo, :]
        if cfg.keys_on_rows:
            v_s[:, :, :, 0:halo] = v_s[:, :, :, rows:rows + halo]
        else:
            v_s[:, :, 0:halo, :] = v_s[:, :, rows:rows + halo, :]
        p_s[:, 0:POOL_HALO, :] = p_s[:, rows:rows + POOL_HALO, :]


def _ffn_kernel(cfg, x_ref, ch_ref, nffn_ref, wup_ref, cw_ref, cb_ref, wdown_ref,
                y_ref, cst_ref, h_s, halo_s):
    ns, rows, tm = cfg.n_seq, cfg.rows, cfg.tile_tokens
    tile = pl.program_id(1)

    @pl.when(tile == 0)
    def _load_history():
        halo_s[...] = ch_ref[...]

    x = x_ref[...].reshape(tm, D_MODEL)
    xnb = _rms_norm(x, nffn_ref[...]).astype(BF16)
    sqrt_half = math.sqrt(0.5)
    n_chunks = D_FF // FF_CHUNK

    def chunk_cols(c, part):
        return slice(part * D_FF + c * FF_CHUNK, part * D_FF + (c + 1) * FF_CHUNK)

    def up_project(c):
        return [jnp.dot(xnb, wup_ref[:, chunk_cols(c, part)], preferred_element_type=F32) for part in range(2)]

    def conv_gelu(c, hs):
        conv = []
        for part, h in enumerate(hs):
            cols = chunk_cols(c, part)
            buf = h_s.at[c % 2, part]
            buf[:, 0:CONV_HALO, :] = halo_s[:, :, cols]
            buf[:, CONV_HALO:CONV_HALO + rows, :] = h.reshape(ns, rows, FF_CHUNK)
            hc = cb_ref[:, cols]
            for j in range(CONV_WIDTH):
                off = CONV_HALO - CONV_HIST + j
                hc = hc + cw_ref[j:j + 1, cols] * buf[:, off:off + rows, :]
            halo_s[:, :, cols] = buf[:, rows:rows + CONV_HALO, :]
            conv.append(hc.reshape(tm, FF_CHUNK))
        gate, val = conv
        gelu = 0.5 * gate * (1.0 + lax.erf(gate * sqrt_half))
        return (gelu * val).astype(BF16)

    y = x
    pending = [up_project(c) for c in range(FF_LOOKAHEAD)]
    for c in range(n_chunks):
        if c + FF_LOOKAHEAD < n_chunks:
            pending.append(up_project(c + FF_LOOKAHEAD))
        act = conv_gelu(c, pending.pop(0))
        y = y + jnp.dot(act, wdown_ref[c * FF_CHUNK:(c + 1) * FF_CHUNK, :], preferred_element_type=F32)
    y_ref[...] = y.reshape(ns, rows, D_MODEL)
    cst_ref[...] = halo_s[...]


def _const_spec(shape):
    return pl.BlockSpec(shape, lambda b, i: (0,) * len(shape))


def _seq_spec(ns, rows, width):
    return pl.BlockSpec((ns, rows, width), lambda b, i: (b, 0, 0))


def _compiler_params():
    return pltpu.CompilerParams(dimension_semantics=("arbitrary", "arbitrary"),
                                vmem_limit_bytes=VMEM_LIMIT_BYTES)


def _mixer_call(cfg, n_groups, x, k_hist, v_hist, p_hist, lp):
    ns, rows, qb, halo, tm = cfg.n_seq, cfg.rows, cfg.q_block, cfg.kv_halo, cfg.tile_tokens
    n_seq_total = n_groups * ns
    assert rows % qb == 0 and cfg.n_early_blocks * qb <= rows, "early blocks must sit in a sequence's first tile"
    assert cfg.keys_on_rows or rows <= CHUNK
    tile_spec = pl.BlockSpec((ns, rows, D_MODEL), lambda b, i: (b, i, 0))
    in_specs = [
        pl.BlockSpec(memory_space=pltpu.SMEM),
        tile_spec,
        _seq_spec(ns, WINDOW, KV_WIDTH), _seq_spec(ns, WINDOW, KV_WIDTH), _seq_spec(ns, POOL_HALO, POOL_WIDTH),
        _const_spec((1, D_MODEL)), _const_spec(lp['w_in'].shape),
        _const_spec((1, LANES)), _const_spec((1, LANES)),
        _const_spec(lp['w_pool'].shape), _const_spec((1, POOL_WIDTH)),
        _const_spec(lp['w_br_attn'].shape), _const_spec(lp['w_br_pool'].shape),
        _const_spec(lp['gate_bias'].shape), _const_spec(lp['w_out'].shape),
    ]
    out_specs = [tile_spec, _seq_spec(ns, WINDOW, KV_WIDTH), _seq_spec(ns, WINDOW, KV_WIDTH),
                 _seq_spec(ns, POOL_HALO, POOL_WIDTH)]
    out_shape = [jax.ShapeDtypeStruct(x.shape, F32),
                 jax.ShapeDtypeStruct((n_seq_total, WINDOW, KV_WIDTH), F32),
                 jax.ShapeDtypeStruct((n_seq_total, WINDOW, KV_WIDTH), F32),
                 jax.ShapeDtypeStruct((n_seq_total, POOL_HALO, POOL_WIDTH), F32)]
    if cfg.keys_on_rows:
        v_shape, a_shape = (N_KV_HEADS, ns, LANES, halo + rows), (ATTN_WIDTH, tm)
        score_shape, sink_shape = (KEY_BLOCK, GROUP * qb), (SUBLANES, GROUP * qb)
    else:
        v_shape, a_shape = (N_KV_HEADS, ns, halo + rows, LANES), (tm, ATTN_WIDTH)
        score_shape, sink_shape = (GROUP * qb, KEY_BLOCK), (GROUP * qb, LANES)
    scratch = [
        pltpu.VMEM((2, tm, ATTN_WIDTH), BF16),
        pltpu.VMEM((N_KV_HEADS, ns, halo + rows, LANES), BF16),
        pltpu.VMEM(v_shape, BF16),
        pltpu.VMEM((ns, POOL_HALO + rows, POOL_WIDTH), F32),
        pltpu.VMEM(a_shape, BF16),
        pltpu.VMEM((2, tm, D_MODEL), F32),
        pltpu.VMEM((1 + cfg.n_early_blocks, N_KV_HEADS) + score_shape, F32),
        pltpu.VMEM((N_KV_HEADS,) + sink_shape, F32),
    ]
    return pl.pallas_call(
        functools.partial(_mixer_kernel, cfg),
        grid=(n_groups, cfg.n_tiles),
        in_specs=in_specs, out_specs=out_specs, out_shape=out_shape, scratch_shapes=scratch,
        compiler_params=_compiler_params(),
        name=f"mixer_r{rows}",
    )(lp['sinks'], x, k_hist, v_hist, p_hist, lp['norm_mix'], lp['w_in'], lp['q_gain'], lp['k_gain'],
      lp['w_pool'], lp['pool_scale'], lp['w_br_attn'], lp['w_br_pool'], lp['gate_bias'], lp['w_out'])


def _ffn_call(cfg, n_groups, x, c_hist, lp):
    ns, rows, tm = cfg.n_seq, cfg.rows, cfg.tile_tokens
    n_seq_total = n_groups * ns
    tile_spec = pl.BlockSpec((ns, rows, D_MODEL), lambda b, i: (b, i, 0))
    in_specs = [
        tile_spec, _seq_spec(ns, CONV_HALO, 2 * D_FF),
        _const_spec((1, D_MODEL)), _const_spec(lp['w_up'].shape),
        _const_spec(lp['conv_w'].shape), _const_spec((1, 2 * D_FF)), _const_spec(lp['w_down'].shape),
    ]
    out_specs = [tile_spec, _seq_spec(ns, CONV_HALO, 2 * D_FF)]
    out_shape = [jax.ShapeDtypeStruct(x.shape, F32),
                 jax.ShapeDtypeStruct((n_seq_total, CONV_HALO, 2 * D_FF), F32)]
    scratch = [
        pltpu.VMEM((2, 2, ns, CONV_HALO + rows, FF_CHUNK), F32),
        pltpu.VMEM((ns, CONV_HALO, 2 * D_FF), F32),
    ]
    return pl.pallas_call(
        functools.partial(_ffn_kernel, cfg),
        grid=(n_groups, cfg.n_tiles),
        in_specs=in_specs, out_specs=out_specs, out_shape=out_shape, scratch_shapes=scratch,
        compiler_params=_compiler_params(),
        name=f"ffn_r{rows}",
    )(x, c_hist, lp['norm_ffn'], lp['w_up'], lp['conv_w'], lp['conv_b'], lp['w_down'])


def _layer(cfg, n_groups, x, k_hist, v_hist, p_hist, c_hist, lp):
    p_pad = jnp.pad(p_hist, ((0, 0), (POOL_HALO - POOL_HIST, 0), (0, 0)))
    c_pad = jnp.pad(c_hist, ((0, 0), (CONV_HALO - CONV_HIST, 0), (0, 0)))
    x, k_state, v_state, p_state = _mixer_call(cfg, n_groups, x, k_hist, v_hist, p_pad, lp)
    x, c_state = _ffn_call(cfg, n_groups, x, c_pad, lp)
    return x, k_state, v_state, p_state[:, POOL_HALO - POOL_HIST:], c_state[:, CONV_HALO - CONV_HIST:]


def kernel(x_prompt, x_sample, cache_k, cache_v, state_pool, state_conv, norm_mix, w_in, q_norm, k_norm,
           sinks, w_pool, pool_scale, w_br_attn, w_br_pool, gate_bias, w_out, norm_ffn, w_up, conv_w,
           conv_b, w_down):
    depth = w_in.shape[0]
    batch, seq = x_prompt.shape[0], x_prompt.shape[1]
    dec_batch, dec_seq = x_sample.shape[0], x_sample.shape[1]
    assert seq % PROMPT_TILE == 0 and dec_seq % SUBLANES == 0 and dec_seq <= CHUNK
    prompt_cfg = TileCfg(n_seq=1, rows=PROMPT_TILE, n_tiles=seq // PROMPT_TILE, pos0=0)
    sample_cfg = TileCfg(n_seq=dec_batch, rows=dec_seq, n_tiles=1, pos0=PAST_LEN)

    score_scale = HEAD_DIM ** -0.5
    zeros_kv = jnp.zeros((batch, WINDOW, KV_WIDTH), F32)
    zeros_pool = jnp.zeros((batch, POOL_HIST, POOL_WIDTH), F32)
    zeros_conv = jnp.zeros((batch, CONV_HIST, 2 * D_FF), F32)

    xp, xs = x_prompt, x_sample
    states_p, states_s = [], []
    for l in range(depth):
        lp = {
            'sinks': sinks[l],
            'norm_mix': norm_mix[l][None, :], 'w_in': w_in[l].astype(BF16),
            'q_gain': (jnp.tile(q_norm[l], 2) * score_scale)[None, :], 'k_gain': jnp.tile(k_norm[l], 2)[None, :],
            'w_pool': w_pool[l].astype(BF16), 'pool_scale': pool_scale[l][None, :],
            'w_br_attn': w_br_attn[l].astype(BF16), 'w_br_pool': w_br_pool[l].astype(BF16),
            'gate_bias': gate_bias[l], 'w_out': w_out[l].astype(BF16),
            'norm_ffn': norm_ffn[l][None, :], 'w_up': w_up[l].astype(BF16), 'conv_w': conv_w[l],
            'conv_b': conv_b[l][None, :], 'w_down': w_down[l].astype(BF16),
        }
        xp, *st_p = _layer(prompt_cfg, batch, xp, zeros_kv, zeros_kv, zeros_pool, zeros_conv, lp)
        xs, *st_s = _layer(sample_cfg, 1, xs,
                           cache_k[l].reshape(dec_batch, WINDOW, KV_WIDTH),
                           cache_v[l].reshape(dec_batch, WINDOW, KV_WIDTH),
                           state_pool[l], state_conv[l], lp)
        states_p.append(st_p)
        states_s.append(st_s)

    def stacked(states, idx, n_seq):
        arr = jnp.stack([st[idx] for st in states], axis=0)
        if idx < 2:
            arr = arr.reshape(depth, n_seq, WINDOW, N_KV_HEADS, HEAD_DIM)
        return arr

    return (xp, xs,
            stacked(states_p, 0, batch), stacked(states_p, 1, batch),
            stacked(states_p, 2, batch), stacked(states_p, 3, batch),
            stacked(states_s, 0, dec_batch), stacked(states_s, 1, dec_batch),
            stacked(states_s, 2, dec_batch), stacked(states_s, 3, dec_batch))
```

```python
import functools
import math
from typing import NamedTuple

import jax
import jax.numpy as jnp
from jax import lax
from jax.experimental import pallas as pl
from jax.experimental.pallas import tpu as pltpu

D_MODEL = 1024
CHUNK = 64
WINDOW = 128
HEAD_DIM = 64
N_Q_HEADS = 8
N_KV_HEADS = 2
GROUP = N_Q_HEADS // N_KV_HEADS
ATTN_WIDTH = N_Q_HEADS * HEAD_DIM
KV_WIDTH = N_KV_HEADS * HEAD_DIM
POOL_WINDOWS = (2, 4, 8, 16)
POOL_WIDTH = 512
POOL_GROUP_DIM = POOL_WIDTH // len(POOL_WINDOWS)
POOL_HIST = max(POOL_WINDOWS) - 1
QKVP_WIDTH = ATTN_WIDTH + 2 * KV_WIDTH + POOL_WIDTH
D_FF = 2816
CONV_WIDTH = 3
CONV_HIST = CONV_WIDTH - 1
PAST_LEN = 1024
EPS = 1e-6
NEG_INF = -1e30

LANES = 128
SUBLANES = 8
KEY_BLOCK = 256
QUERY_BLOCK = 128
POOL_HALO = 16
CONV_HALO = SUBLANES
FF_CHUNK = 256
FF_LOOKAHEAD = 2
PROMPT_TILE = 512
VMEM_LIMIT_BYTES = 56 * 1024 * 1024

F32 = jnp.float32
BF16 = jnp.bfloat16


class TileCfg(NamedTuple):
    n_groups: int
    n_seq: int
    rows: int
    n_tiles: int
    pos0: int

    @property
    def n_steps(self):
        return self.n_groups * self.n_tiles

    @property
    def pipelined(self):
        return self.n_steps > 1

    @property
    def q_block(self):
        return min(QUERY_BLOCK, self.rows)

    @property
    def keys_on_rows(self):
        return self.rows >= QUERY_BLOCK

    @property
    def kv_halo(self):
        return KEY_BLOCK - self.q_block

    @property
    def tile_tokens(self):
        return self.n_seq * self.rows

    @property
    def n_early_blocks(self):
        return max(0, -(-(WINDOW - self.pos0) // self.q_block))


def _rms_norm(x, gain):
    ms = jnp.mean(x * x, axis=-1, keepdims=True)
    return x * lax.rsqrt(ms + EPS) * gain


def _half_lane_mask(shape):
    return lax.broadcasted_iota(jnp.int32, shape, len(shape) - 1) < HEAD_DIM


def _head_pair_rms_norm(x, gain):
    low = _half_lane_mask(x.shape)
    sq = x * x
    ms_lo = jnp.sum(jnp.where(low, sq, 0.0), axis=-1, keepdims=True) * (1.0 / HEAD_DIM)
    ms_hi = jnp.sum(jnp.where(low, 0.0, sq), axis=-1, keepdims=True) * (1.0 / HEAD_DIM)
    inv = jnp.where(low, lax.rsqrt(ms_lo + EPS), lax.rsqrt(ms_hi + EPS))
    return x * inv * gain


def _dup_halves(x):
    low = _half_lane_mask(x.shape)
    swapped = pltpu.roll(x, HEAD_DIM, axis=x.ndim - 1)
    return jnp.where(low, x, swapped), jnp.where(low, swapped, x)


def _build_attention_tables(cfg, sinks_ref, bias_s, sink_s):
    qb, halo = cfg.q_block, cfg.kv_halo
    q_axis, k_axis = (1, 0) if cfg.keys_on_rows else (0, 1)
    shape = (KEY_BLOCK, GROUP * qb) if cfg.keys_on_rows else (GROUP * qb, KEY_BLOCK)
    stacked = lax.broadcasted_iota(jnp.int32, shape, q_axis)
    s = lax.broadcasted_iota(jnp.int32, shape, k_axis)
    head_in_group = stacked // qb
    t = stacked % qb
    dist = jnp.abs((halo + t) - s).astype(F32)
    chunk_start = halo + (t // CHUNK) * CHUNK
    in_window = jnp.logical_and(s >= chunk_start - WINDOW, s < chunk_start + CHUNK)
    sink_shape = (SUBLANES, GROUP * qb) if cfg.keys_on_rows else (GROUP * qb, LANES)
    sink_head = lax.broadcasted_iota(jnp.int32, sink_shape, q_axis) // qb
    for kh in range(N_KV_HEADS):
        slope = jnp.zeros(shape, F32)
        sink = jnp.zeros(sink_shape, F32)
        for g in range(GROUP):
            h = kh * GROUP + g
            slope = jnp.where(head_in_group == g, 2.0 ** (-8.0 * (h + 1) / N_Q_HEADS), slope)
            sink = jnp.where(sink_head == g, sinks_ref[h], sink)
        sink_s[kh] = sink
        for v in range(1 + cfg.n_early_blocks):
            visible = in_window if v == 0 else jnp.logical_and(in_window, s >= halo - cfg.pos0 - (v - 1) * qb)
            bias_s[v, kh] = jnp.where(visible, -slope * dist, NEG_INF)


def _mixer_kernel(cfg, sinks_ref, x_ref, xn_ref, kh_ref, vh_ref, ph_ref, nmix_ref, win_ref, qg_ref, kg_ref,
                  wpool_ref, pscale_ref, wba_ref, wbp_ref, gbias_ref, wout_ref,
                  y_ref, kst_ref, vst_ref, pst_ref,
                  xnb_s, qkvp_s, q_s, k_s, v_s, p_s, a_s, g_s, bias_s, sink_s):
    ns, rows, qb, halo = cfg.n_seq, cfg.rows, cfg.q_block, cfg.kv_halo
    tm = cfg.tile_tokens
    tile = pl.program_id(1)

    @pl.when(jnp.logical_and(pl.program_id(0) == 0, tile == 0))
    def _first_call():
        _build_attention_tables(cfg, sinks_ref, bias_s, sink_s)

    def store_kv(k_val, v_val, row_lo, n):
        for kh, (kd, vd) in enumerate(zip(_dup_halves(k_val), _dup_halves(v_val))):
            k_s[kh, :, row_lo:row_lo + n, :] = kd.astype(BF16).reshape(ns, n, LANES)
            if cfg.keys_on_rows:
                for seq in range(ns):
                    v_s[kh, seq, :, row_lo:row_lo + n] = vd[seq * n:(seq + 1) * n].T.astype(BF16)
            else:
                v_s[kh, :, row_lo:row_lo + n, :] = vd.astype(BF16).reshape(ns, n, LANES)

    @pl.when(tile == 0)
    def _load_history():
        if halo > WINDOW:
            zeros = jnp.zeros((ns * (halo - WINDOW), LANES), F32)
            store_kv(zeros, zeros, 0, halo - WINDOW)
        store_kv(kh_ref[...].reshape(ns * WINDOW, LANES), vh_ref[...].reshape(ns * WINDOW, LANES),
                 halo - WINDOW, WINDOW)
        p_s[:, 0:POOL_HALO, :] = ph_ref[...]

    def normed(src_ref):
        return _rms_norm(src_ref[...].reshape(tm, D_MODEL), nmix_ref[...]).astype(BF16)

    def project(xn, part):
        cols = slice(0, ATTN_WIDTH) if part == 0 else slice(ATTN_WIDTH, QKVP_WIDTH)
        return cols, jnp.dot(xn, win_ref[:, cols], preferred_element_type=F32)

    def store_first_stage(xn, parts):
        xnb_s[...] = xn
        for cols, val in parts:
            qkvp_s[:, cols] = val

    @pl.when(jnp.logical_and(pl.program_id(0) == 0, tile == 0))
    def _first_step():
        xn = normed(x_ref)
        store_first_stage(xn, [project(xn, 0), project(xn, 1)])

    xnb = xnb_s[...]
    qkvp = qkvp_s

    def gate(idx):
        cols = slice(QKVP_WIDTH + idx * D_MODEL, QKVP_WIDTH + (idx + 1) * D_MODEL)
        g_s[idx] = jax.nn.sigmoid(jnp.dot(xnb, win_ref[:, cols], preferred_element_type=F32)
                                  + gbias_ref[idx:idx + 1, :])

    gate(0)

    low_t = _half_lane_mask((tm, LANES))
    for j in range(ATTN_WIDTH // LANES):
        qn = _head_pair_rms_norm(qkvp[:, j * LANES:(j + 1) * LANES], qg_ref[...])
        q_s[0, :, j * LANES:(j + 1) * LANES] = jnp.where(low_t, qn, 0.0).astype(BF16)
        q_s[1, :, j * LANES:(j + 1) * LANES] = jnp.where(low_t, 0.0, qn).astype(BF16)
    kn = _head_pair_rms_norm(qkvp[:, ATTN_WIDTH:ATTN_WIDTH + KV_WIDTH], kg_ref[...])
    vv = qkvp[:, ATTN_WIDTH + KV_WIDTH:ATTN_WIDTH + 2 * KV_WIDTH]
    pin = qkvp[:, ATTN_WIDTH + 2 * KV_WIDTH:QKVP_WIDTH]
    store_kv(kn, vv, halo, rows)
    p_s[:, POOL_HALO:POOL_HALO + rows, :] = pin.reshape(ns, rows, POOL_WIDTH)

    if rows >= WINDOW:
        kst_ref[...] = kn.reshape(ns, rows, LANES)[:, rows - WINDOW:, :]
        vst_ref[...] = vv.reshape(ns, rows, LANES)[:, rows - WINDOW:, :]
    else:
        kst_ref[:, 0:WINDOW - rows, :] = kh_ref[:, rows:, :]
        vst_ref[:, 0:WINDOW - rows, :] = vh_ref[:, rows:, :]
        kst_ref[:, WINDOW - rows:, :] = kn.reshape(ns, rows, LANES)
        vst_ref[:, WINDOW - rows:, :] = vv.reshape(ns, rows, LANES)

    red_axis = 0 if cfg.keys_on_rows else 1
    units = [(seq, block, kh) for seq in range(ns) for block in range(rows // qb) for kh in range(N_KV_HEADS)]

    def sink_of(kh):
        return sink_s[kh][0:1, :] if cfg.keys_on_rows else sink_s[kh][:, 0:1]

    def scores(seq, block, kh):
        row0, tok0 = block * qb, seq * rows + block * qb
        table = jnp.where(tile == 0, block + 1, 0) if block < cfg.n_early_blocks else 0
        qstack = jnp.concatenate(
            [q_s[half, tok0:tok0 + qb, j * LANES:(j + 1) * LANES]
             for j in (2 * kh, 2 * kh + 1) for half in range(2)], axis=0)
        kc = k_s[kh, seq, row0:row0 + KEY_BLOCK, :]
        lhs, rhs = (kc, qstack) if cfg.keys_on_rows else (qstack, kc)
        sc = lax.dot_general(lhs, rhs, (((1,), (1,)), ((), ())), preferred_element_type=F32)
        return sc + bias_s[table, kh]

    def weights(sc, kh):
        sink = sink_of(kh)
        m = jnp.maximum(jnp.max(sc, axis=red_axis, keepdims=True), sink)
        p = jnp.exp(sc - m)
        inv = 1.0 / (jnp.sum(p, axis=red_axis, keepdims=True) + jnp.exp(sink - m))
        return p.astype(BF16), inv

    def outputs(seq, block, kh, p, inv):
        row0, tok0 = block * qb, seq * rows + block * qb
        if cfg.keys_on_rows:
            o = jnp.dot(v_s[kh, seq, :, row0:row0 + KEY_BLOCK], p, preferred_element_type=F32) * inv
            low = lax.broadcasted_iota(jnp.int32, (LANES, qb), 0) < HEAD_DIM
            for jj in range(2):
                pair = jnp.where(low, o[:, (2 * jj) * qb:(2 * jj + 1) * qb],
                                 o[:, (2 * jj + 1) * qb:(2 * jj + 2) * qb])
                r0 = (2 * kh + jj) * LANES
                a_s[r0:r0 + LANES, tok0:tok0 + qb] = pair.astype(BF16)
        else:
            o = jnp.dot(p, v_s[kh, seq, row0:row0 + KEY_BLOCK, :], preferred_element_type=F32) * inv
            low = _half_lane_mask((qb, LANES))
            for jj in range(2):
                pair = jnp.where(low, o[(2 * jj) * qb:(2 * jj + 1) * qb],
                                 o[(2 * jj + 1) * qb:(2 * jj + 2) * qb])
                c0 = (2 * kh + jj) * LANES
                a_s[tok0:tok0 + qb, c0:c0 + LANES] = pair.astype(BF16)

    def pool_mixer():
        row_pos = cfg.pos0 + tile * rows + lax.broadcasted_iota(jnp.int32, (rows, 1), 0)
        pooled = []
        for g, w in enumerate(POOL_WINDOWS):
            cols = slice(g * POOL_GROUP_DIM, (g + 1) * POOL_GROUP_DIM)
            win = p_s[:, POOL_HALO:POOL_HALO + rows, cols]
            for j in range(1, w):
                win = win + p_s[:, POOL_HALO - j:POOL_HALO - j + rows, cols]
            if cfg.pos0 >= POOL_HIST:
                mean = win * (1.0 / w)
            else:
                mean = win * (1.0 / jnp.minimum(row_pos + 1, w).astype(F32))
            d = (mean - p_s[:, POOL_HALO:POOL_HALO + rows, cols]).reshape(tm, POOL_GROUP_DIM)
            yg = jnp.dot(d.astype(BF16), wpool_ref[g], preferred_element_type=F32)
            pooled.append((yg * pscale_ref[:, cols]).astype(BF16))
        return jnp.concatenate(pooled, axis=-1)

    all_scores = [scores(*u) for u in units]
    gate(1)
    plb = pool_mixer()
    pst_ref[...] = p_s[:, rows:rows + POOL_HALO, :]
    yb = jnp.dot(plb, wbp_ref[...], preferred_element_type=F32)
    all_weights = [weights(sc, u[2]) for sc, u in zip(all_scores, units)]
    for u, (p, inv) in zip(units, all_weights):
        outputs(*u, p, inv)

    next_parts = []
    if cfg.pipelined:
        xnb_next = normed(xn_ref)
        next_parts.append(project(xnb_next, 0))
    if cfg.keys_on_rows:
        ya = lax.dot_general(a_s[...], wba_ref[...], (((0,), (0,)), ((), ())), preferred_element_type=F32)
    else:
        ya = jnp.dot(a_s[...], wba_ref[...], preferred_element_type=F32)
    if cfg.pipelined:
        next_parts.append(project(xnb_next, 1))
    merged = g_s[0] * ya + g_s[1] * yb
    mix = jnp.dot(merged.astype(BF16), wout_ref[...], preferred_element_type=F32)
    y_ref[...] = x_ref[...] + mix.reshape(ns, rows, D_MODEL)
    if cfg.pipelined:
        store_first_stage(xnb_next, next_parts)

    if cfg.n_tiles > 1:
        k_s[:, :, 0:halo, :] = k_s[:, :, rows:rows + halo, :]
        if cfg.keys_on_rows:
            v_s[:, :, :, 0:halo] = v_s[:, :, :, rows:rows + halo]
        else:
            v_s[:, :, 0:halo, :] = v_s[:, :, rows:rows + halo, :]
        p_s[:, 0:POOL_HALO, :] = p_s[:, rows:rows + POOL_HALO, :]


def _ffn_kernel(cfg, x_ref, xn_ref, ch_ref, nffn_ref, wup_ref, cw_ref, cb_ref, wdown_ref,
                y_ref, cst_ref, h_s, halo_s, xnb_s, head_s):
    ns, rows, tm = cfg.n_seq, cfg.rows, cfg.tile_tokens
    tile = pl.program_id(1)
    sqrt_half = math.sqrt(0.5)
    n_chunks = D_FF // FF_CHUNK

    def chunk_cols(c, part):
        return slice(part * D_FF + c * FF_CHUNK, part * D_FF + (c + 1) * FF_CHUNK)

    def normed(src_ref):
        return _rms_norm(src_ref[...].reshape(tm, D_MODEL), nffn_ref[...]).astype(BF16)

    def up_project(xn, c):
        return [jnp.dot(xn, wup_ref[:, chunk_cols(c, part)], preferred_element_type=F32) for part in range(2)]

    def store_head(xn, heads):
        xnb_s[...] = xn
        for c, hs in enumerate(heads):
            for part, h in enumerate(hs):
                head_s[c, part] = h

    @pl.when(jnp.logical_and(pl.program_id(0) == 0, tile == 0))
    def _first_step():
        xn = normed(x_ref)
        store_head(xn, [up_project(xn, c) for c in range(FF_LOOKAHEAD)])

    @pl.when(tile == 0)
    def _load_history():
        halo_s[...] = ch_ref[...]

    xnb = xnb_s[...]

    def conv_gelu(c, hs):
        conv = []
        for part, h in enumerate(hs):
            cols = chunk_cols(c, part)
            buf = h_s.at[c % 2, part]
            buf[:, 0:CONV_HALO, :] = halo_s[:, :, cols]
            buf[:, CONV_HALO:CONV_HALO + rows, :] = h.reshape(ns, rows, FF_CHUNK)
            hc = cb_ref[:, cols]
            for j in range(CONV_WIDTH):
                off = CONV_HALO - CONV_HIST + j
                hc = hc + cw_ref[j:j + 1, cols] * buf[:, off:off + rows, :]
            halo_s[:, :, cols] = buf[:, rows:rows + CONV_HALO, :]
            conv.append(hc.reshape(tm, FF_CHUNK))
        gate, val = conv
        gelu = 0.5 * gate * (1.0 + lax.erf(gate * sqrt_half))
        return (gelu * val).astype(BF16)

    def down_project(acc, act, c):
        return acc + jnp.dot(act, wdown_ref[c * FF_CHUNK:(c + 1) * FF_CHUNK, :], preferred_element_type=F32)

    y = x_ref[...].reshape(tm, D_MODEL)
    act = None
    pending =[[head_s[c, part] for part in range(2)] for c in range(FF_LOOKAHEAD)]
    xnb_next, next_heads = None, []
    for c in range(n_chunks):
        for ahead in ([FF_LOOKAHEAD, FF_LOOKAHEAD + 1] if c == 0 else [c + FF_LOOKAHEAD + 1]):
            if ahead < n_chunks:
                pending.append(up_project(xnb, ahead))
            elif cfg.pipelined and ahead - n_chunks < FF_LOOKAHEAD:
                if xnb_next is None:
                    xnb_next = normed(xn_ref)
                next_heads.append(up_project(xnb_next, ahead - n_chunks))
        if c > 0:
            y = down_project(y, act, c - 1)
        act = conv_gelu(c, pending.pop(0))
    y = down_project(y, act, n_chunks - 1)
    y_ref[...] = y.reshape(ns, rows, D_MODEL)
    cst_ref[...] = halo_s[...]
    if cfg.pipelined:
        store_head(xnb_next, next_heads)


def _const_spec(shape):
    return pl.BlockSpec(shape, lambda b, i: (0,) * len(shape))


def _seq_spec(ns, rows, width):
    return pl.BlockSpec((ns, rows, width), lambda b, i: (b, 0, 0))


def _compiler_params():
    return pltpu.CompilerParams(dimension_semantics=("arbitrary", "arbitrary"),
                                vmem_limit_bytes=VMEM_LIMIT_BYTES)


def _tile_specs(cfg):
    block = (cfg.n_seq, cfg.rows, D_MODEL)

    def next_tile(b, i):
        step = jnp.minimum(b * cfg.n_tiles + i + 1, cfg.n_steps - 1)
        return (step // cfg.n_tiles, step % cfg.n_tiles, 0)

    return pl.BlockSpec(block, lambda b, i: (b, i, 0)), pl.BlockSpec(block, next_tile)


def _mixer_call(cfg, x, k_hist, v_hist, p_hist, lp):
    ns, rows, qb, halo, tm = cfg.n_seq, cfg.rows, cfg.q_block, cfg.kv_halo, cfg.tile_tokens
    n_seq_total = cfg.n_groups * ns
    assert rows % qb == 0 and cfg.n_early_blocks * qb <= rows, "early blocks must sit in a sequence's first tile"
    assert cfg.keys_on_rows or rows <= CHUNK
    tile_spec, next_tile_spec = _tile_specs(cfg)
    in_specs = [
        pl.BlockSpec(memory_space=pltpu.SMEM),
        tile_spec, next_tile_spec,
        _seq_spec(ns, WINDOW, KV_WIDTH), _seq_spec(ns, WINDOW, KV_WIDTH), _seq_spec(ns, POOL_HALO, POOL_WIDTH),
        _const_spec((1, D_MODEL)), _const_spec(lp['w_in'].shape),
        _const_spec((1, LANES)), _const_spec((1, LANES)),
        _const_spec(lp['w_pool'].shape), _const_spec((1, POOL_WIDTH)),
        _const_spec(lp['w_br_attn'].shape), _const_spec(lp['w_br_pool'].shape),
        _const_spec(lp['gate_bias'].shape), _const_spec(lp['w_out'].shape),
    ]
    out_specs = [tile_spec, _seq_spec(ns, WINDOW, KV_WIDTH), _seq_spec(ns, WINDOW, KV_WIDTH),
                 _seq_spec(ns, POOL_HALO, POOL_WIDTH)]
    out_shape = [jax.ShapeDtypeStruct(x.shape, F32),
                 jax.ShapeDtypeStruct((n_seq_total, WINDOW, KV_WIDTH), F32),
                 jax.ShapeDtypeStruct((n_seq_total, WINDOW, KV_WIDTH), F32),
                 jax.ShapeDtypeStruct((n_seq_total, POOL_HALO, POOL_WIDTH), F32)]
    if cfg.keys_on_rows:
        v_shape, a_shape = (N_KV_HEADS, ns, LANES, halo + rows), (ATTN_WIDTH, tm)
        score_shape, sink_shape = (KEY_BLOCK, GROUP * qb), (SUBLANES, GROUP * qb)
    else:
        v_shape, a_shape = (N_KV_HEADS, ns, halo + rows, LANES), (tm, ATTN_WIDTH)
        score_shape, sink_shape = (GROUP * qb, KEY_BLOCK), (GROUP * qb, LANES)
    scratch = [
        pltpu.VMEM((tm, D_MODEL), BF16),
        pltpu.VMEM((tm, QKVP_WIDTH), F32),
        pltpu.VMEM((2, tm, ATTN_WIDTH), BF16),
        pltpu.VMEM((N_KV_HEADS, ns, halo + rows, LANES), BF16),
        pltpu.VMEM(v_shape, BF16),
        pltpu.VMEM((ns, POOL_HALO + rows, POOL_WIDTH), F32),
        pltpu.VMEM(a_shape, BF16),
        pltpu.VMEM((2, tm, D_MODEL), F32),
        pltpu.VMEM((1 + cfg.n_early_blocks, N_KV_HEADS) + score_shape, F32),
        pltpu.VMEM((N_KV_HEADS,) + sink_shape, F32),
    ]
    return pl.pallas_call(
        functools.partial(_mixer_kernel, cfg),
        grid=(cfg.n_groups, cfg.n_tiles),
        in_specs=in_specs, out_specs=out_specs, out_shape=out_shape, scratch_shapes=scratch,
        compiler_params=_compiler_params(),
        name=f"mixer_r{rows}",
    )(lp['sinks'], x, x, k_hist, v_hist, p_hist, lp['norm_mix'], lp['w_in'], lp['q_gain'], lp['k_gain'],
      lp['w_pool'], lp['pool_scale'], lp['w_br_attn'], lp['w_br_pool'], lp['gate_bias'], lp['w_out'])


def _ffn_call(cfg, x, c_hist, lp):
    ns, rows, tm = cfg.n_seq, cfg.rows, cfg.tile_tokens
    n_seq_total = cfg.n_groups * ns
    tile_spec, next_tile_spec = _tile_specs(cfg)
    in_specs = [
        tile_spec, next_tile_spec, _seq_spec(ns, CONV_HALO, 2 * D_FF),
        _const_spec((1, D_MODEL)), _const_spec(lp['w_up'].shape),
        _const_spec(lp['conv_w'].shape), _const_spec((1, 2 * D_FF)), _const_spec(lp['w_down'].shape),
    ]
    out_specs = [tile_spec, _seq_spec(ns, CONV_HALO, 2 * D_FF)]
    out_shape = [jax.ShapeDtypeStruct(x.shape, F32),
                 jax.ShapeDtypeStruct((n_seq_total, CONV_HALO, 2 * D_FF), F32)]
    scratch = [
        pltpu.VMEM((2, 2, ns, CONV_HALO + rows, FF_CHUNK), F32),
        pltpu.VMEM((ns, CONV_HALO, 2 * D_FF), F32),
        pltpu.VMEM((tm, D_MODEL), BF16),
        pltpu.VMEM((FF_LOOKAHEAD, 2, tm, FF_CHUNK), F32),
    ]
    return pl.pallas_call(
        functools.partial(_ffn_kernel, cfg),
        grid=(cfg.n_groups, cfg.n_tiles),
        in_specs=in_specs, out_specs=out_specs, out_shape=out_shape, scratch_shapes=scratch,
        compiler_params=_compiler_params(),
        name=f"ffn_r{rows}",
    )(x, x, c_hist, lp['norm_ffn'], lp['w_up'], lp['conv_w'], lp['conv_b'], lp['w_down'])


def _layer(cfg, x, k_hist, v_hist, p_hist, c_hist, lp):
    p_pad = jnp.pad(p_hist, ((0, 0), (POOL_HALO - POOL_HIST, 0), (0, 0)))
    c_pad = jnp.pad(c_hist, ((0, 0), (CONV_HALO - CONV_HIST, 0), (0, 0)))
    x, k_state, v_state, p_state = _mixer_call(cfg, x, k_hist, v_hist, p_pad, lp)
    x, c_state = _ffn_call(cfg, x, c_pad, lp)
    return x, k_state, v_state, p_state[:, POOL_HALO - POOL_HIST:], c_state[:, CONV_HALO - CONV_HIST:]


def kernel(x_prompt, x_sample, cache_k, cache_v, state_pool, state_conv, norm_mix, w_in, q_norm, k_norm,
           sinks, w_pool, pool_scale, w_br_attn, w_br_pool, gate_bias, w_out, norm_ffn, w_up, conv_w,
           conv_b, w_down):
    depth = w_in.shape[0]
    batch, seq = x_prompt.shape[0], x_prompt.shape[1]
    dec_batch, dec_seq = x_sample.shape[0], x_sample.shape[1]
    assert seq % PROMPT_TILE == 0 and dec_seq % SUBLANES == 0 and dec_seq <= CHUNK
    prompt_cfg = TileCfg(n_groups=batch, n_seq=1, rows=PROMPT_TILE, n_tiles=seq // PROMPT_TILE, pos0=0)
    sample_cfg = TileCfg(n_groups=1, n_seq=dec_batch, rows=dec_seq, n_tiles=1, pos0=PAST_LEN)

    score_scale = HEAD_DIM ** -0.5
    zeros_kv = jnp.zeros((batch, WINDOW, KV_WIDTH), F32)
    zeros_pool = jnp.zeros((batch, POOL_HIST, POOL_WIDTH), F32)
    zeros_conv = jnp.zeros((batch, CONV_HIST, 2 * D_FF), F32)

    xp, xs = x_prompt, x_sample
    states_p, states_s = [], []
    for l in range(depth):
        lp = {
            'sinks': sinks[l],
            'norm_mix': norm_mix[l][None, :], 'w_in': w_in[l].astype(BF16),
            'q_gain': (jnp.tile(q_norm[l], 2) * score_scale)[None, :], 'k_gain': jnp.tile(k_norm[l], 2)[None, :],
            'w_pool': w_pool[l].astype(BF16), 'pool_scale': pool_scale[l][None, :],
            'w_br_attn': w_br_attn[l].astype(BF16), 'w_br_pool': w_br_pool[l].astype(BF16),
            'gate_bias': gate_bias[l], 'w_out': w_out[l].astype(BF16),
            'norm_ffn': norm_ffn[l][None, :], 'w_up': w_up[l].astype(BF16), 'conv_w': conv_w[l],
            'conv_b': conv_b[l][None, :], 'w_down': w_down[l].astype(BF16),
        }
        xp, *st_p = _layer(prompt_cfg, xp, zeros_kv, zeros_kv, zeros_pool, zeros_conv, lp)
        xs, *st_s = _layer(sample_cfg, xs,
                           cache_k[l].reshape(dec_batch, WINDOW, KV_WIDTH),
                           cache_v[l].reshape(dec_batch, WINDOW, KV_WIDTH),
                           state_pool[l], state_conv[l], lp)
        states_p.append(st_p)
        states_s.append(st_s)

    def stacked(states, idx, n_seq):
        arr = jnp.stack([st[idx] for st in states], axis=0)
        if idx < 2:
            arr = arr.reshape(depth, n_seq, WINDOW, N_KV_HEADS, HEAD_DIM)
        return arr

    return (xp, xs,
            stacked(states_p, 0, batch), stacked(states_p, 1, batch),
            stacked(states_p, 2, batch), stacked(states_p, 3, batch),
            stacked(states_s, 0, dec_batch), stacked(states_s, 1, dec_batch),
            stacked(states_s, 2, dec_batch), stacked(states_s, 3, dec_batch))
```

```python
import functools
import math
from typing import NamedTuple

import jax
import jax.numpy as jnp
from jax import lax
from jax.experimental import pallas as pl
from jax.experimental.pallas import tpu as pltpu

D_MODEL = 1024
CHUNK = 64
WINDOW = 128
HEAD_DIM = 64
N_Q_HEADS = 8
N_KV_HEADS = 2
GROUP = N_Q_HEADS // N_KV_HEADS
ATTN_WIDTH = N_Q_HEADS * HEAD_DIM
KV_WIDTH = N_KV_HEADS * HEAD_DIM
POOL_WINDOWS = (2, 4, 8, 16)
POOL_WIDTH = 512
POOL_GROUP_DIM = POOL_WIDTH // len(POOL_WINDOWS)
POOL_HIST = max(POOL_WINDOWS) - 1
QKVP_WIDTH = ATTN_WIDTH + 2 * KV_WIDTH + POOL_WIDTH
D_FF = 2816
CONV_WIDTH = 3
CONV_HIST = CONV_WIDTH - 1
PAST_LEN = 1024
EPS = 1e-6
NEG_INF = -1e30

LANES = 128
SUBLANES = 8
KEY_BLOCK = 256
QUERY_BLOCK = 128
POOL_HALO = 16
CONV_HALO = SUBLANES
FF_CHUNK = 256
FF_LOOKAHEAD = 3
MIXER_TILE = 512
FFN_TILE = 256
SUB_TILE = 256
VMEM_LIMIT_BYTES = 56 * 1024 * 1024

F32 = jnp.float32
BF16 = jnp.bfloat16


class TileCfg(NamedTuple):
    n_groups: int
    n_seq: int
    rows: int
    n_tiles: int
    pos0: int

    @property
    def sub_rows(self):
        return min(SUB_TILE, self.rows)

    @property
    def n_sub(self):
        return self.rows // self.sub_rows

    @property
    def q_block(self):
        return min(QUERY_BLOCK, self.rows)

    @property
    def keys_on_rows(self):
        return self.rows >= QUERY_BLOCK

    @property
    def kv_halo(self):
        return KEY_BLOCK - self.q_block

    @property
    def n_early_blocks(self):
        return max(0, -(-(WINDOW - self.pos0) // self.q_block))


def _rms_norm(x, gain):
    ms = jnp.mean(x * x, axis=-1, keepdims=True)
    return x * lax.rsqrt(ms + EPS) * gain


def _half_lane_mask(shape):
    return lax.broadcasted_iota(jnp.int32, shape, len(shape) - 1) < HEAD_DIM


def _head_pair_rms_norm(x, gain):
    low = _half_lane_mask(x.shape)
    sq = x * x
    ms_lo = jnp.sum(jnp.where(low, sq, 0.0), axis=-1, keepdims=True) * (1.0 / HEAD_DIM)
    ms_hi = jnp.sum(jnp.where(low, 0.0, sq), axis=-1, keepdims=True) * (1.0 / HEAD_DIM)
    inv = jnp.where(low, lax.rsqrt(ms_lo + EPS), lax.rsqrt(ms_hi + EPS))
    return x * inv * gain


def _dup_halves(x):
    low = _half_lane_mask(x.shape)
    swapped = pltpu.roll(x, HEAD_DIM, axis=x.ndim - 1)
    return jnp.where(low, x, swapped), jnp.where(low, swapped, x)


def _interleave(stage_lists, order):
    gens = [iter(s) for s in stage_lists]
    for idx in order:
        next(gens[idx])
    for g in gens:
        for _ in g:
            pass


def _build_attention_tables(cfg, layer, sinks_ref, bias_s, sink_s):
    qb, halo = cfg.q_block, cfg.kv_halo
    q_axis, k_axis = (1, 0) if cfg.keys_on_rows else (0, 1)
    shape = (KEY_BLOCK, GROUP * qb) if cfg.keys_on_rows else (GROUP * qb, KEY_BLOCK)
    stacked = lax.broadcasted_iota(jnp.int32, shape, q_axis)
    s = lax.broadcasted_iota(jnp.int32, shape, k_axis)
    head_in_group = stacked // qb
    t = stacked % qb
    dist = jnp.abs((halo + t) - s).astype(F32)
    chunk_start = halo + (t // CHUNK) * CHUNK
    in_window = jnp.logical_and(s >= chunk_start - WINDOW, s < chunk_start + CHUNK)
    sink_shape = (SUBLANES, GROUP * qb) if cfg.keys_on_rows else (GROUP * qb, LANES)
    sink_head = lax.broadcasted_iota(jnp.int32, sink_shape, q_axis) // qb
    for kh in range(N_KV_HEADS):
        slope = jnp.zeros(shape, F32)
        sink = jnp.zeros(sink_shape, F32)
        for g in range(GROUP):
            h = kh * GROUP + g
            slope = jnp.where(head_in_group == g, 2.0 ** (-8.0 * (h + 1) / N_Q_HEADS), slope)
            sink = jnp.where(sink_head == g, sinks_ref[layer, h], sink)
        sink_s[kh] = sink
        for v in range(1 + cfg.n_early_blocks):
            visible = in_window if v == 0 else jnp.logical_and(in_window, s >= halo - cfg.pos0 - (v - 1) * qb)
            bias_s[v, kh] = jnp.where(visible, -slope * dist, NEG_INF)


def _mixer_kernel(cfg, layer, sinks_ref, x_ref, kh_ref, vh_ref, ph_ref, nmix_ref, win_ref, qg_ref, kg_ref,
                  wpool_ref, pscale_ref, wba_ref, wbp_ref, gbias_ref, wout_ref,
                  y_ref, kst_ref, vst_ref, pst_ref,
                  q_s, k_s, v_s, p_s, a_s, g_s, bias_s, sink_s):
    ns, rows, sub, qb, halo = cfg.n_seq, cfg.rows, cfg.sub_rows, cfg.q_block, cfg.kv_halo
    tm = ns * sub
    tile = pl.program_id(1)
    red_axis = 0 if cfg.keys_on_rows else 1

    @pl.when(jnp.logical_and(pl.program_id(0) == 0, tile == 0))
    def _first_call():
        _build_attention_tables(cfg, layer, sinks_ref, bias_s, sink_s)

    def store_kv(k_val, v_val, row_lo, n):
        for kh, (kd, vd) in enumerate(zip(_dup_halves(k_val), _dup_halves(v_val))):
            k_s[kh, :, row_lo:row_lo + n, :] = kd.astype(BF16).reshape(ns, n, LANES)
            if cfg.keys_on_rows:
                for seq in range(ns):
                    v_s[kh, seq, :, row_lo:row_lo + n] = vd[seq * n:(seq + 1) * n].T.astype(BF16)
            else:
                v_s[kh, :, row_lo:row_lo + n, :] = vd.astype(BF16).reshape(ns, n, LANES)

    @pl.when(tile == 0)
    def _load_history():
        if halo > WINDOW:
            zeros = jnp.zeros((ns * (halo - WINDOW), LANES), F32)
            store_kv(zeros, zeros, 0, halo - WINDOW)
        store_kv(kh_ref[...].reshape(ns * WINDOW, LANES), vh_ref[...].reshape(ns * WINDOW, LANES),
                 halo - WINDOW, WINDOW)
        p_s[:, 0:POOL_HALO, :] = ph_ref[...]

    def sub_tile(si):
        r0 = si * sub
        t0 = si * tm
        last = si == cfg.n_sub - 1
        x = x_ref[:, r0:r0 + sub, :].reshape(tm, D_MODEL)
        xnb = _rms_norm(x, nmix_ref[...]).astype(BF16)
        qkvp = jnp.dot(xnb, win_ref[:, 0:QKVP_WIDTH], preferred_element_type=F32)

        def gate(idx):
            cols = slice(QKVP_WIDTH + idx * D_MODEL, QKVP_WIDTH + (idx + 1) * D_MODEL)
            g_s[idx, t0:t0 + tm, :] = jax.nn.sigmoid(
                jnp.dot(xnb, win_ref[:, cols], preferred_element_type=F32) + gbias_ref[idx:idx + 1, :])

        gate(0)

        low_t = _half_lane_mask((tm, LANES))
        for j in range(ATTN_WIDTH // LANES):
            qn = _head_pair_rms_norm(qkvp[:, j * LANES:(j + 1) * LANES], qg_ref[...])
            q_s[0, t0:t0 + tm, j * LANES:(j + 1) * LANES] = jnp.where(low_t, qn, 0.0).astype(BF16)
            q_s[1, t0:t0 + tm, j * LANES:(j + 1) * LANES] = jnp.where(low_t, 0.0, qn).astype(BF16)
        kn = _head_pair_rms_norm(qkvp[:, ATTN_WIDTH:ATTN_WIDTH + KV_WIDTH], kg_ref[...])
        vv = qkvp[:, ATTN_WIDTH + KV_WIDTH:ATTN_WIDTH + 2 * KV_WIDTH]
        pin = qkvp[:, ATTN_WIDTH + 2 * KV_WIDTH:QKVP_WIDTH]
        store_kv(kn, vv, halo + r0, sub)
        p_s[:, POOL_HALO + r0:POOL_HALO + r0 + sub, :] = pin.reshape(ns, sub, POOL_WIDTH)

        if sub >= WINDOW:
            if last:
                kst_ref[...] = kn.reshape(ns, sub, LANES)[:, sub - WINDOW:, :]
                vst_ref[...] = vv.reshape(ns, sub, LANES)[:, sub - WINDOW:, :]
        else:
            kst_ref[:, 0:WINDOW - rows, :] = kh_ref[:, rows:, :]
            vst_ref[:, 0:WINDOW - rows, :] = vh_ref[:, rows:, :]
            kst_ref[:, WINDOW - rows:, :] = kn.reshape(ns, rows, LANES)
            vst_ref[:, WINDOW - rows:, :] = vv.reshape(ns, rows, LANES)

        units = [(seq, r0 // qb + blk, kh) for seq in range(ns) for blk in range(sub // qb)
                 for kh in range(N_KV_HEADS)]

        def sink_of(kh):
            return sink_s[kh][0:1, :] if cfg.keys_on_rows else sink_s[kh][:, 0:1]

        def scores(seq, block, kh):
            row0 = block * qb
            tok0 = (seq * sub if ns > 1 else 0) + row0
            table = jnp.where(tile == 0, block + 1, 0) if block < cfg.n_early_blocks else 0
            qstack = jnp.concatenate(
                [q_s[half, tok0:tok0 + qb, j * LANES:(j + 1) * LANES]
                 for j in (2 * kh, 2 * kh + 1) for half in range(2)], axis=0)
            kc = k_s[kh, seq, row0:row0 + KEY_BLOCK, :]
            lhs, rhs = (kc, qstack) if cfg.keys_on_rows else (qstack, kc)
            sc = lax.dot_general(lhs, rhs, (((1,), (1,)), ((), ())), preferred_element_type=F32)
            return sc + bias_s[table, kh]

        def weights(sc, kh):
            sink = sink_of(kh)
            m = jnp.maximum(jnp.max(sc, axis=red_axis, keepdims=True), sink)
            p = jnp.exp(sc - m)
            inv = 1.0 / (jnp.sum(p, axis=red_axis, keepdims=True) + jnp.exp(sink - m))
            return p.astype(BF16), inv

        def outputs(seq, block, kh, p, inv):
            row0 = block * qb
            tok0 = (seq * sub if ns > 1 else 0) + row0
            if cfg.keys_on_rows:
                o = jnp.dot(v_s[kh, seq, :, row0:row0 + KEY_BLOCK], p, preferred_element_type=F32) * inv
                low = lax.broadcasted_iota(jnp.int32, (LANES, qb), 0) < HEAD_DIM
                for jj in range(2):
                    pair = jnp.where(low, o[:, (2 * jj) * qb:(2 * jj + 1) * qb],
                                     o[:, (2 * jj + 1) * qb:(2 * jj + 2) * qb])
                    c0 = (2 * kh + jj) * LANES
                    a_s[c0:c0 + LANES, tok0:tok0 + qb] = pair.astype(BF16)
            else:
                o = jnp.dot(p, v_s[kh, seq, row0:row0 + KEY_BLOCK, :], preferred_element_type=F32) * inv
                low = _half_lane_mask((qb, LANES))
                for jj in range(2):
                    pair = jnp.where(low, o[(2 * jj) * qb:(2 * jj + 1) * qb],
                                     o[(2 * jj + 1) * qb:(2 * jj + 2) * qb])
                    c0 = (2 * kh + jj) * LANES
                    a_s[tok0:tok0 + qb, c0:c0 + LANES] = pair.astype(BF16)

        def pool_mixer():
            row_pos = cfg.pos0 + tile * rows + r0 + lax.broadcasted_iota(jnp.int32, (sub, 1), 0)
            base = POOL_HALO + r0
            pooled = []
            for g, w in enumerate(POOL_WINDOWS):
                cols = slice(g * POOL_GROUP_DIM, (g + 1) * POOL_GROUP_DIM)
                win = p_s[:, base:base + sub, cols]
                for j in range(1, w):
                    win = win + p_s[:, base - j:base - j + sub, cols]
                if cfg.pos0 >= POOL_HIST:
                    mean = win * (1.0 / w)
                else:
                    mean = win * (1.0 / jnp.minimum(row_pos + 1, w).astype(F32))
                d = (mean - p_s[:, base:base + sub, cols]).reshape(tm, POOL_GROUP_DIM)
                yg = jnp.dot(d.astype(BF16), wpool_ref[g], preferred_element_type=F32)
                pooled.append((yg * pscale_ref[:, cols]).astype(BF16))
            return jnp.concatenate(pooled, axis=-1)

        all_scores = [scores(*u) for u in units]
        yield
        gate(1)
        plb = pool_mixer()
        if last:
            pst_ref[...] = p_s[:, rows:rows + POOL_HALO, :]
        yb = jnp.dot(plb, wbp_ref[...], preferred_element_type=F32)
        yield
        all_weights = [weights(sc, u[2]) for sc, u in zip(all_scores, units)]
        for u, (p, inv) in zip(units, all_weights):
            outputs(*u, p, inv)
        yield
        if cfg.keys_on_rows:
            ya = lax.dot_general(a_s[:, t0:t0 + tm], wba_ref[...], (((0,), (0,)), ((), ())),
                                 preferred_element_type=F32)
        else:
            ya = jnp.dot(a_s[t0:t0 + tm, :], wba_ref[...], preferred_element_type=F32)
        yield
        merged = g_s[0, t0:t0 + tm, :] * ya + g_s[1, t0:t0 + tm, :] * yb
        mix = jnp.dot(merged.astype(BF16), wout_ref[...], preferred_element_type=F32)
        y_ref[:, r0:r0 + sub, :] = x_ref[:, r0:r0 + sub, :] + mix.reshape(ns, sub, D_MODEL)
        yield

    order = [0, 0, 0]
    for si in range(1, cfg.n_sub):
        order += [si, si - 1, si, si - 1, si]
    order += [cfg.n_sub - 1, cfg.n_sub - 1]
    _interleave([sub_tile(si) for si in range(cfg.n_sub)], order)

    if cfg.n_tiles > 1:
        k_s[:, :, 0:halo, :] = k_s[:, :, rows:rows + halo, :]
        if cfg.keys_on_rows:
            v_s[:, :, :, 0:halo] = v_s[:, :, :, rows:rows + halo]
        else:
            v_s[:, :, 0:halo, :] = v_s[:, :, rows:rows + halo, :]
        p_s[:, 0:POOL_HALO, :] = p_s[:, rows:rows + POOL_HALO, :]


def _ffn_kernel(cfg, x_ref, ch_ref, nffn_ref, wup_ref, cw_ref, cb_ref, wdown_ref,
                y_ref, cst_ref, h_s, halo_s):
    ns, rows, sub = cfg.n_seq, cfg.rows, cfg.sub_rows
    tm = ns * sub
    tile = pl.program_id(1)
    sqrt_half = math.sqrt(0.5)
    n_chunks = D_FF // FF_CHUNK

    @pl.when(tile == 0)
    def _load_history():
        halo_s[...] = ch_ref[...]

    def chunk_cols(c, part):
        return slice(part * D_FF + c * FF_CHUNK, part * D_FF + (c + 1) * FF_CHUNK)

    def sub_tile(si):
        r0 = si * sub
        x = x_ref[:, r0:r0 + sub, :].reshape(tm, D_MODEL)
        xnb = _rms_norm(x, nffn_ref[...]).astype(BF16)

        def up_project(c):
            return [jnp.dot(xnb, wup_ref[:, chunk_cols(c, part)], preferred_element_type=F32)
                    for part in range(2)]

        def conv_gelu(c, hs):
            conv = []
            for part, h in enumerate(hs):
                cols = chunk_cols(c, part)
                buf = h_s.at[c % 2, part]
                buf[:, 0:CONV_HALO, :] = halo_s[:, :, cols]
                buf[:, CONV_HALO:CONV_HALO + sub, :] = h.reshape(ns, sub, FF_CHUNK)
                hc = cb_ref[:, cols]
                for j in range(CONV_WIDTH):
                    off = CONV_HALO - CONV_HIST + j
                    hc = hc + cw_ref[j:j + 1, cols] * buf[:, off:off + sub, :]
                halo_s[:, :, cols] = buf[:, sub:sub + CONV_HALO, :]
                conv.append(hc.reshape(tm, FF_CHUNK))
            gate, val = conv
            gelu = 0.5 * gate * (1.0 + lax.erf(gate * sqrt_half))
            return (gelu * val).astype(BF16)

        pending = [up_project(c) for c in range(FF_LOOKAHEAD)]
        yield
        y = None
        for c in range(n_chunks):
            if c + FF_LOOKAHEAD < n_chunks:
                pending.append(up_project(c + FF_LOOKAHEAD))
            act = conv_gelu(c, pending.pop(0))
            part = jnp.dot(act, wdown_ref[c * FF_CHUNK:(c + 1) * FF_CHUNK, :], preferred_element_type=F32)
            y = part if y is None else y + part
            yield
        y_ref[:, r0:r0 + sub, :] = x_ref[:, r0:r0 + sub, :] + y.reshape(ns, sub, D_MODEL)
        yield

    order = []
    for si in range(cfg.n_sub):
        body = [si] * (n_chunks + 1)
        if si + 1 < cfg.n_sub:
            body.insert(n_chunks - FF_LOOKAHEAD, si + 1)
        order += ([si] if si == 0 else []) + body
    _interleave([sub_tile(si) for si in range(cfg.n_sub)], order)
    cst_ref[...] = halo_s[...]


def _layer_spec(shape):
    return lambda layer: pl.BlockSpec((None,) + tuple(shape), lambda b, i: (layer,) + (0,) * len(shape))


def _hist_spec(ns, rows, width):
    return lambda layer: pl.BlockSpec((None, ns, rows, width), lambda b, i: (layer, b, 0, 0))


def _seq_spec(ns, rows, width):
    return pl.BlockSpec((ns, rows, width), lambda b, i: (b, 0, 0))


def _compiler_params():
    return pltpu.CompilerParams(dimension_semantics=("arbitrary", "arbitrary"),
                                vmem_limit_bytes=VMEM_LIMIT_BYTES)


def _mixer_call(cfg, layer, hist_layer, x, hist, prm):
    ns, rows, sub, qb, halo = cfg.n_seq, cfg.rows, cfg.sub_rows, cfg.q_block, cfg.kv_halo
    n_seq_total = cfg.n_groups * ns
    assert rows % sub == 0 and sub % qb == 0 and (ns == 1 or cfg.n_sub == 1)
    assert cfg.n_early_blocks * qb <= rows, "early blocks must sit in a sequence's first tile"
    assert cfg.keys_on_rows or rows <= CHUNK
    tile_spec = pl.BlockSpec((ns, rows, D_MODEL), lambda b, i: (b, i, 0))
    in_specs = [
        pl.BlockSpec(memory_space=pltpu.SMEM), tile_spec,
        _hist_spec(ns, WINDOW, KV_WIDTH)(hist_layer), _hist_spec(ns, WINDOW, KV_WIDTH)(hist_layer),
        _hist_spec(ns, POOL_HALO, POOL_WIDTH)(hist_layer),
    ] + [_layer_spec(prm[name].shape[1:])(layer) for name in
         ('norm_mix', 'w_in', 'q_gain', 'k_gain', 'w_pool', 'pool_scale', 'w_br_attn', 'w_br_pool',
          'gate_bias', 'w_out')]
    out_specs = [tile_spec, _seq_spec(ns, WINDOW, KV_WIDTH), _seq_spec(ns, WINDOW, KV_WIDTH),
                 _seq_spec(ns, POOL_HALO, POOL_WIDTH)]
    out_shape = [jax.ShapeDtypeStruct(x.shape, F32),
                 jax.ShapeDtypeStruct((n_seq_total, WINDOW, KV_WIDTH), F32),
                 jax.ShapeDtypeStruct((n_seq_total, WINDOW, KV_WIDTH), F32),
                 jax.ShapeDtypeStruct((n_seq_total, POOL_HALO, POOL_WIDTH), F32)]
    tokens = ns * rows
    if cfg.keys_on_rows:
        v_shape, a_shape = (N_KV_HEADS, ns, LANES, halo + rows), (ATTN_WIDTH, tokens)
        score_shape, sink_shape = (KEY_BLOCK, GROUP * qb), (SUBLANES, GROUP * qb)
    else:
        v_shape, a_shape = (N_KV_HEADS, ns, halo + rows, LANES), (tokens, ATTN_WIDTH)
        score_shape, sink_shape = (GROUP * qb, KEY_BLOCK), (GROUP * qb, LANES)
    scratch = [
        pltpu.VMEM((2, tokens, ATTN_WIDTH), BF16),
        pltpu.VMEM((N_KV_HEADS, ns, halo + rows, LANES), BF16),
        pltpu.VMEM(v_shape, BF16),
        pltpu.VMEM((ns, POOL_HALO + rows, POOL_WIDTH), F32),
        pltpu.VMEM(a_shape, BF16),
        pltpu.VMEM((2, tokens, D_MODEL), F32),
        pltpu.VMEM((1 + cfg.n_early_blocks, N_KV_HEADS) + score_shape, F32),
        pltpu.VMEM((N_KV_HEADS,) + sink_shape, F32),
    ]
    return pl.pallas_call(
        functools.partial(_mixer_kernel, cfg, layer),
        grid=(cfg.n_groups, cfg.n_tiles),
        in_specs=in_specs, out_specs=out_specs, out_shape=out_shape, scratch_shapes=scratch,
        compiler_params=_compiler_params(),
        name=f"mixer_r{rows}",
    )(prm['sinks'], x, hist['k'], hist['v'], hist['pool'], prm['norm_mix'], prm['w_in'], prm['q_gain'],
      prm['k_gain'], prm['w_pool'], prm['pool_scale'], prm['w_br_attn'], prm['w_br_pool'], prm['gate_bias'],
      prm['w_out'])


def _ffn_call(cfg, layer, hist_layer, x, hist, prm):
    ns, rows, sub = cfg.n_seq, cfg.rows, cfg.sub_rows
    n_seq_total = cfg.n_groups * ns
    tile_spec = pl.BlockSpec((ns, rows, D_MODEL), lambda b, i: (b, i, 0))
    in_specs = [tile_spec, _hist_spec(ns, CONV_HALO, 2 * D_FF)(hist_layer)] + [
        _layer_spec(prm[name].shape[1:])(layer) for name in ('norm_ffn', 'w_up', 'conv_w', 'conv_b', 'w_down')]
    out_specs = [tile_spec, _seq_spec(ns, CONV_HALO, 2 * D_FF)]
    out_shape = [jax.ShapeDtypeStruct(x.shape, F32),
                 jax.ShapeDtypeStruct((n_seq_total, CONV_HALO, 2 * D_FF), F32)]
    scratch = [
        pltpu.VMEM((2, 2, ns, CONV_HALO + sub, FF_CHUNK), F32),
        pltpu.VMEM((ns, CONV_HALO, 2 * D_FF), F32),
    ]
    return pl.pallas_call(
        functools.partial(_ffn_kernel, cfg),
        grid=(cfg.n_groups, cfg.n_tiles),
        in_specs=in_specs, out_specs=out_specs, out_shape=out_shape, scratch_shapes=scratch,
        compiler_params=_compiler_params(),
        name=f"ffn_r{rows}",
    )(x, hist['conv'], prm['norm_ffn'], prm['w_up'], prm['conv_w'], prm['conv_b'], prm['w_down'])


def _layer(mixer_cfg, ffn_cfg, layer, hist_layer, x, hist, prm):
    x, k_state, v_state, p_state = _mixer_call(mixer_cfg, layer, hist_layer, x, hist, prm)
    x, c_state = _ffn_call(ffn_cfg, layer, hist_layer, x, hist, prm)
    return x, k_state, v_state, p_state[:, POOL_HALO - POOL_HIST:], c_state[:, CONV_HALO - CONV_HIST:]


def kernel(x_prompt, x_sample, cache_k, cache_v, state_pool, state_conv, norm_mix, w_in, q_norm, k_norm,
           sinks, w_pool, pool_scale, w_br_attn, w_br_pool, gate_bias, w_out, norm_ffn, w_up, conv_w,
           conv_b, w_down):
    depth = w_in.shape[0]
    batch, seq = x_prompt.shape[0], x_prompt.shape[1]
    dec_batch, dec_seq = x_sample.shape[0], x_sample.shape[1]
    assert seq % MIXER_TILE == 0 and seq % FFN_TILE == 0 and dec_seq % SUBLANES == 0 and dec_seq <= CHUNK
    prompt_cfgs = [TileCfg(n_groups=batch, n_seq=1, rows=t, n_tiles=seq // t, pos0=0) for t in (MIXER_TILE, FFN_TILE)]
    sample_cfg = TileCfg(n_groups=1, n_seq=dec_batch, rows=dec_seq, n_tiles=1, pos0=PAST_LEN)

    row = lambda a: a[:, None, :]
    prm = {
        'sinks': sinks,
        'norm_mix': row(norm_mix), 'w_in': w_in.astype(BF16),
        'q_gain': row(jnp.tile(q_norm, (1, 2)) * HEAD_DIM ** -0.5), 'k_gain': row(jnp.tile(k_norm, (1, 2))),
        'w_pool': w_pool.astype(BF16), 'pool_scale': row(pool_scale),
        'w_br_attn': w_br_attn.astype(BF16), 'w_br_pool': w_br_pool.astype(BF16),
        'gate_bias': gate_bias, 'w_out': w_out.astype(BF16),
        'norm_ffn': row(norm_ffn), 'w_up': w_up.astype(BF16), 'conv_w': conv_w,
        'conv_b': row(conv_b), 'w_down': w_down.astype(BF16),
    }
    prompt_hist = {
        'k': jnp.zeros((1, batch, WINDOW, KV_WIDTH), F32), 'v': jnp.zeros((1, batch, WINDOW, KV_WIDTH), F32),
        'pool': jnp.zeros((1, batch, POOL_HALO, POOL_WIDTH), F32),
        'conv': jnp.zeros((1, batch, CONV_HALO, 2 * D_FF), F32),
    }
    sample_hist = {
        'k': cache_k.reshape(depth, dec_batch, WINDOW, KV_WIDTH),
        'v': cache_v.reshape(depth, dec_batch, WINDOW, KV_WIDTH),
        'pool': jnp.pad(state_pool, ((0, 0), (0, 0), (POOL_HALO - POOL_HIST, 0), (0, 0))),
        'conv': jnp.pad(state_conv, ((0, 0), (0, 0), (CONV_HALO - CONV_HIST, 0), (0, 0))),
    }

    xp, xs = x_prompt, x_sample
    states_p, states_s = [], []
    for l in range(depth):
        xp, *st_p = _layer(*prompt_cfgs, l, 0, xp, prompt_hist, prm)
        xs, *st_s = _layer(sample_cfg, sample_cfg, l, l, xs, sample_hist, prm)
        states_p.append(st_p)
        states_s.append(st_s)

    def stacked(states, idx, n_seq):
        arr = jnp.stack([st[idx] for st in states], axis=0)
        if idx < 2:
            arr = arr.reshape(depth, n_seq, WINDOW, N_KV_HEADS, HEAD_DIM)
        return arr

    return (xp, xs,
            stacked(states_p, 0, batch), stacked(states_p, 1, batch),
            stacked(states_p, 2, batch), stacked(states_p, 3, batch),
            stacked(states_s, 0, dec_batch), stacked(states_s, 1, dec_batch),
            stacked(states_s, 2, dec_batch), stacked(states_s, 3, dec_batch))
```

```python
import functools
import math
from typing import NamedTuple

import jax
import jax.numpy as jnp
from jax import lax
from jax.experimental import pallas as pl
from jax.experimental.pallas import tpu as pltpu

D_MODEL = 1024
CHUNK = 64
WINDOW = 128
HEAD_DIM = 64
N_Q_HEADS = 8
N_KV_HEADS = 2
GROUP = N_Q_HEADS // N_KV_HEADS
ATTN_WIDTH = N_Q_HEADS * HEAD_DIM
KV_WIDTH = N_KV_HEADS * HEAD_DIM
POOL_WINDOWS = (2, 4, 8, 16)
POOL_WIDTH = 512
POOL_GROUP_DIM = POOL_WIDTH // len(POOL_WINDOWS)
POOL_HIST = max(POOL_WINDOWS) - 1
QKVP_WIDTH = ATTN_WIDTH + 2 * KV_WIDTH + POOL_WIDTH
D_FF = 2816
CONV_WIDTH = 3
CONV_HIST = CONV_WIDTH - 1
PAST_LEN = 1024
EPS = 1e-6
NEG_INF = -1e30

LANES = 128
SUBLANES = 8
KEY_BLOCK = 256
QUERY_BLOCK = 128
POOL_HALO = 16
CONV_HALO = SUBLANES
FF_CHUNK = 256
FF_LOOKAHEAD = 3
N_CONV_BUFS = FF_LOOKAHEAD + 1
MIXER_TILE = 512
FFN_TILE = 256
SUB_TILE = 256
VMEM_LIMIT_BYTES = 56 * 1024 * 1024

F32 = jnp.float32
BF16 = jnp.bfloat16


class TileCfg(NamedTuple):
    n_groups: int
    n_seq: int
    rows: int
    n_tiles: int
    pos0: int

    @property
    def sub_rows(self):
        return min(SUB_TILE, self.rows)

    @property
    def n_sub(self):
        return self.rows // self.sub_rows

    @property
    def q_block(self):
        return min(QUERY_BLOCK, self.rows)

    @property
    def keys_on_rows(self):
        return self.rows >= QUERY_BLOCK

    @property
    def kv_halo(self):
        return KEY_BLOCK - self.q_block

    @property
    def n_early_blocks(self):
        return max(0, -(-(WINDOW - self.pos0) // self.q_block))


def _rms_norm(x, gain):
    ms = jnp.mean(x * x, axis=-1, keepdims=True)
    return x * lax.rsqrt(ms + EPS) * gain


def _half_lane_mask(shape):
    return lax.broadcasted_iota(jnp.int32, shape, len(shape) - 1) < HEAD_DIM


def _head_pair_rms_norm(x, gain):
    low = _half_lane_mask(x.shape)
    sq = x * x
    ms_lo = jnp.sum(jnp.where(low, sq, 0.0), axis=-1, keepdims=True) * (1.0 / HEAD_DIM)
    ms_hi = jnp.sum(jnp.where(low, 0.0, sq), axis=-1, keepdims=True) * (1.0 / HEAD_DIM)
    inv = jnp.where(low, lax.rsqrt(ms_lo + EPS), lax.rsqrt(ms_hi + EPS))
    return x * inv * gain


def _dup_halves(x):
    low = _half_lane_mask(x.shape)
    swapped = pltpu.roll(x, HEAD_DIM, axis=x.ndim - 1)
    return jnp.where(low, x, swapped), jnp.where(low, swapped, x)


def _rows(ref, r0, n, cols=slice(None)):
    if ref.shape[0] == 1:
        return ref[0, r0:r0 + n, cols]
    val = ref[:, r0:r0 + n, cols]
    return val.reshape(ref.shape[0] * n, val.shape[-1])


def _set_rows(ref, r0, n, val, cols=slice(None)):
    if ref.shape[0] == 1:
        ref[0, r0:r0 + n, cols] = val
    else:
        ref[:, r0:r0 + n, cols] = val.reshape(ref.shape[0], n, val.shape[-1])


def _shift_rows(x, k):
    return pltpu.roll(x, k, axis=0)


def _interleave(stage_lists, order):
    gens = [iter(s) for s in stage_lists]
    for idx in order:
        next(gens[idx])
    for g in gens:
        for _ in g:
            pass


def _build_attention_tables(cfg, layer, sinks_ref, bias_s, sink_s):
    qb, halo = cfg.q_block, cfg.kv_halo
    q_axis, k_axis = (1, 0) if cfg.keys_on_rows else (0, 1)
    shape = (KEY_BLOCK, GROUP * qb) if cfg.keys_on_rows else (GROUP * qb, KEY_BLOCK)
    stacked = lax.broadcasted_iota(jnp.int32, shape, q_axis)
    s = lax.broadcasted_iota(jnp.int32, shape, k_axis)
    head_in_group = stacked // qb
    t = stacked % qb
    dist = jnp.abs((halo + t) - s).astype(F32)
    chunk_start = halo + (t // CHUNK) * CHUNK
    in_window = jnp.logical_and(s >= chunk_start - WINDOW, s < chunk_start + CHUNK)
    sink_shape = (SUBLANES, GROUP * qb) if cfg.keys_on_rows else (GROUP * qb, LANES)
    sink_head = lax.broadcasted_iota(jnp.int32, sink_shape, q_axis) // qb
    for kh in range(N_KV_HEADS):
        slope = jnp.zeros(shape, F32)
        sink = jnp.zeros(sink_shape, F32)
        for g in range(GROUP):
            h = kh * GROUP + g
            slope = jnp.where(head_in_group == g, 2.0 ** (-8.0 * (h + 1) / N_Q_HEADS), slope)
            sink = jnp.where(sink_head == g, sinks_ref[layer, h], sink)
        sink_s[kh] = sink
        for v in range(1 + cfg.n_early_blocks):
            visible = in_window if v == 0 else jnp.logical_and(in_window, s >= halo - cfg.pos0 - (v - 1) * qb)
            bias_s[v, kh] = jnp.where(visible, -slope * dist, NEG_INF)


def _mixer_kernel(cfg, layer, sinks_ref, x_ref, kh_ref, vh_ref, ph_ref, nmix_ref, win_ref, qg_ref, kg_ref,
                  wpool_ref, pscale_ref, wba_ref, wbp_ref, gbias_ref, wout_ref,
                  y_ref, kst_ref, vst_ref, pst_ref,
                  q_s, k_s, v_s, p_s, a_s, g_s, bias_s, sink_s):
    ns, rows, sub, qb, halo = cfg.n_seq, cfg.rows, cfg.sub_rows, cfg.q_block, cfg.kv_halo
    tm = ns * sub
    tile = pl.program_id(1)
    red_axis = 0 if cfg.keys_on_rows else 1

    @pl.when(jnp.logical_and(pl.program_id(0) == 0, tile == 0))
    def _first_call():
        _build_attention_tables(cfg, layer, sinks_ref, bias_s, sink_s)

    def store_kv(k_val, v_val, row_lo, n):
        for kh, (kd, vd) in enumerate(zip(_dup_halves(k_val), _dup_halves(v_val))):
            k_s[kh, :, row_lo:row_lo + n, :] = kd.astype(BF16).reshape(ns, n, LANES)
            if cfg.keys_on_rows:
                for seq in range(ns):
                    v_s[kh, seq, :, row_lo:row_lo + n] = vd[seq * n:(seq + 1) * n].T.astype(BF16)
            else:
                v_s[kh, :, row_lo:row_lo + n, :] = vd.astype(BF16).reshape(ns, n, LANES)

    @pl.when(tile == 0)
    def _load_history():
        if halo > WINDOW:
            zeros = jnp.zeros((ns * (halo - WINDOW), LANES), F32)
            store_kv(zeros, zeros, 0, halo - WINDOW)
        store_kv(kh_ref[...].reshape(ns * WINDOW, LANES), vh_ref[...].reshape(ns * WINDOW, LANES),
                 halo - WINDOW, WINDOW)
        p_s[:, 0:POOL_HALO, :] = ph_ref[...]

    def sub_tile(si):
        r0 = si * sub
        t0 = si * tm
        last = si == cfg.n_sub - 1
        xnb = _rms_norm(_rows(x_ref, r0, sub), nmix_ref[...]).astype(BF16)
        qkvp = jnp.dot(xnb, win_ref[:, 0:QKVP_WIDTH], preferred_element_type=F32)

        def gate(idx):
            cols = slice(QKVP_WIDTH + idx * D_MODEL, QKVP_WIDTH + (idx + 1) * D_MODEL)
            g_s[idx, t0:t0 + tm, :] = jax.nn.sigmoid(
                jnp.dot(xnb, win_ref[:, cols], preferred_element_type=F32) + gbias_ref[idx:idx + 1, :])

        gate(0)

        low_t = _half_lane_mask((tm, LANES))
        for j in range(ATTN_WIDTH // LANES):
            qn = _head_pair_rms_norm(qkvp[:, j * LANES:(j + 1) * LANES], qg_ref[...])
            q_s[0, t0:t0 + tm, j * LANES:(j + 1) * LANES] = jnp.where(low_t, qn, 0.0).astype(BF16)
            q_s[1, t0:t0 + tm, j * LANES:(j + 1) * LANES] = jnp.where(low_t, 0.0, qn).astype(BF16)
        kn = _head_pair_rms_norm(qkvp[:, ATTN_WIDTH:ATTN_WIDTH + KV_WIDTH], kg_ref[...])
        vv = qkvp[:, ATTN_WIDTH + KV_WIDTH:ATTN_WIDTH + 2 * KV_WIDTH]
        pin = qkvp[:, ATTN_WIDTH + 2 * KV_WIDTH:QKVP_WIDTH]
        store_kv(kn, vv, halo + r0, sub)
        _set_rows(p_s, POOL_HALO + r0, sub, pin)

        if sub >= WINDOW:
            if last:
                for seq in range(ns):
                    kst_ref[seq] = kn[(seq + 1) * sub - WINDOW:(seq + 1) * sub]
                    vst_ref[seq] = vv[(seq + 1) * sub - WINDOW:(seq + 1) * sub]
        else:
            kst_ref[:, 0:WINDOW - rows, :] = kh_ref[:, rows:, :]
            vst_ref[:, 0:WINDOW - rows, :] = vh_ref[:, rows:, :]
            _set_rows(kst_ref, WINDOW - rows, rows, kn)
            _set_rows(vst_ref, WINDOW - rows, rows, vv)

        units = [(seq, r0 // qb + blk, kh) for seq in range(ns) for blk in range(sub // qb)
                 for kh in range(N_KV_HEADS)]

        def sink_of(kh):
            return sink_s[kh][0:1, :] if cfg.keys_on_rows else sink_s[kh][:, 0:1]

        def scores(seq, block, kh):
            row0 = block * qb
            tok0 = (seq * sub if ns > 1 else 0) + row0
            table = jnp.where(tile == 0, block + 1, 0) if block < cfg.n_early_blocks else 0
            qstack = jnp.concatenate(
                [q_s[half, tok0:tok0 + qb, j * LANES:(j + 1) * LANES]
                 for j in (2 * kh, 2 * kh + 1) for half in range(2)], axis=0)
            kc = k_s[kh, seq, row0:row0 + KEY_BLOCK, :]
            lhs, rhs = (kc, qstack) if cfg.keys_on_rows else (qstack, kc)
            sc = lax.dot_general(lhs, rhs, (((1,), (1,)), ((), ())), preferred_element_type=F32)
            return sc + bias_s[table, kh]

        def weights(sc, kh):
            sink = sink_of(kh)
            m = jnp.maximum(jnp.max(sc, axis=red_axis, keepdims=True), sink)
            p = jnp.exp(sc - m)
            inv = 1.0 / (jnp.sum(p, axis=red_axis, keepdims=True) + jnp.exp(sink - m))
            return p.astype(BF16), inv

        def outputs(seq, block, kh, p, inv):
            row0 = block * qb
            tok0 = (seq * sub if ns > 1 else 0) + row0
            if cfg.keys_on_rows:
                o = jnp.dot(v_s[kh, seq, :, row0:row0 + KEY_BLOCK], p, preferred_element_type=F32) * inv
                low = lax.broadcasted_iota(jnp.int32, (LANES, qb), 0) < HEAD_DIM
                for jj in range(2):
                    pair = jnp.where(low, o[:, (2 * jj) * qb:(2 * jj + 1) * qb],
                                     o[:, (2 * jj + 1) * qb:(2 * jj + 2) * qb])
                    c0 = (2 * kh + jj) * LANES
                    a_s[c0:c0 + LANES, tok0:tok0 + qb] = pair.astype(BF16)
            else:
                o = jnp.dot(p, v_s[kh, seq, row0:row0 + KEY_BLOCK, :], preferred_element_type=F32) * inv
                low = _half_lane_mask((qb, LANES))
                for jj in range(2):
                    pair = jnp.where(low, o[(2 * jj) * qb:(2 * jj + 1) * qb],
                                     o[(2 * jj + 1) * qb:(2 * jj + 2) * qb])
                    c0 = (2 * kh + jj) * LANES
                    a_s[tok0:tok0 + qb, c0:c0 + LANES] = pair.astype(BF16)

        def pool_mixer():
            row_pos = cfg.pos0 + tile * rows + r0 + lax.broadcasted_iota(jnp.int32, (sub, 1), 0)
            pooled = []
            for g, w in enumerate(POOL_WINDOWS):
                cols = slice(g * POOL_GROUP_DIM, (g + 1) * POOL_GROUP_DIM)
                diffs = []
                for seq in range(ns):
                    ext = p_s[seq, r0:r0 + POOL_HALO + sub, cols]
                    acc, span = ext, 1
                    while span < w:
                        acc = acc + _shift_rows(acc, span)
                        span *= 2
                    if cfg.pos0 >= POOL_HIST:
                        mean = acc[POOL_HALO:] * (1.0 / w)
                    else:
                        mean = acc[POOL_HALO:] * (1.0 / jnp.minimum(row_pos + 1, w).astype(F32))
                    diffs.append(mean - ext[POOL_HALO:])
                d = jnp.concatenate(diffs, axis=0) if ns > 1 else diffs[0]
                yg = jnp.dot(d.astype(BF16), wpool_ref[g], preferred_element_type=F32)
                pooled.append((yg * pscale_ref[:, cols]).astype(BF16))
            return jnp.concatenate(pooled, axis=-1)

        all_scores = [scores(*u) for u in units]
        yield
        gate(1)
        plb = pool_mixer()
        if last:
            pst_ref[...] = p_s[:, rows:rows + POOL_HALO, :]
        yb = jnp.dot(plb, wbp_ref[...], preferred_element_type=F32)
        yield
        all_weights = [weights(sc, u[2]) for sc, u in zip(all_scores, units)]
        for u, (p, inv) in zip(units, all_weights):
            outputs(*u, p, inv)
        yield
        if cfg.keys_on_rows:
            ya = lax.dot_general(a_s[:, t0:t0 + tm], wba_ref[...], (((0,), (0,)), ((), ())),
                                 preferred_element_type=F32)
        else:
            ya = jnp.dot(a_s[t0:t0 + tm, :], wba_ref[...], preferred_element_type=F32)
        yield
        merged = g_s[0, t0:t0 + tm, :] * ya + g_s[1, t0:t0 + tm, :] * yb
        mix = jnp.dot(merged.astype(BF16), wout_ref[...], preferred_element_type=F32)
        _set_rows(y_ref, r0, sub, _rows(x_ref, r0, sub) + mix)
        yield

    order = [0, 0, 0]
    for si in range(1, cfg.n_sub):
        order += [si, si - 1, si, si - 1, si]
    order += [cfg.n_sub - 1, cfg.n_sub - 1]
    _interleave([sub_tile(si) for si in range(cfg.n_sub)], order)

    if cfg.n_tiles > 1:
        k_s[:, :, 0:halo, :] = k_s[:, :, rows:rows + halo, :]
        if cfg.keys_on_rows:
            v_s[:, :, :, 0:halo] = v_s[:, :, :, rows:rows + halo]
        else:
            v_s[:, :, 0:halo, :] = v_s[:, :, rows:rows + halo, :]
        p_s[:, 0:POOL_HALO, :] = p_s[:, rows:rows + POOL_HALO, :]


def _ffn_kernel(cfg, x_ref, ch_ref, nffn_ref, wup_ref, cw_ref, cb_ref, wdown_ref,
                y_ref, cst_ref, h_s, halo_s):
    ns, rows, sub = cfg.n_seq, cfg.rows, cfg.sub_rows
    tm = ns * sub
    tile = pl.program_id(1)
    sqrt_half = math.sqrt(0.5)
    n_chunks = D_FF // FF_CHUNK

    @pl.when(tile == 0)
    def _load_history():
        halo_s[...] = ch_ref[...]

    def chunk_cols(c, part):
        return slice(part * D_FF + c * FF_CHUNK, part * D_FF + (c + 1) * FF_CHUNK)

    def sub_tile(si):
        r0 = si * sub
        xnb = _rms_norm(_rows(x_ref, r0, sub), nffn_ref[...]).astype(BF16)

        def up_project(c):
            for part in range(2):
                h = jnp.dot(xnb, wup_ref[:, chunk_cols(c, part)], preferred_element_type=F32)
                _set_rows(h_s.at[c % N_CONV_BUFS, part], CONV_HALO, sub, h)

        def conv_gelu(c):
            conv = []
            for part in range(2):
                cols = chunk_cols(c, part)
                buf = h_s.at[c % N_CONV_BUFS, part]
                buf[:, 0:CONV_HALO, :] = halo_s[:, :, cols]
                taps = []
                for seq in range(ns):
                    ext = buf[seq, 0:CONV_HALO + sub, :]
                    acc = cw_ref[0:1, cols] * ext
                    for j in range(1, CONV_WIDTH):
                        acc = cw_ref[j:j + 1, cols] * ext + _shift_rows(acc, 1)
                    taps.append(cb_ref[:, cols] + acc[CONV_HALO:])
                halo_s[:, :, cols] = buf[:, sub:sub + CONV_HALO, :]
                conv.append(jnp.concatenate(taps, axis=0) if ns > 1 else taps[0])
            gate, val = conv
            gelu = 0.5 * gate * (1.0 + lax.erf(gate * sqrt_half))
            return (gelu * val).astype(BF16)

        for c in range(FF_LOOKAHEAD):
            up_project(c)
        yield
        y = None
        for c in range(n_chunks):
            if c + FF_LOOKAHEAD < n_chunks:
                up_project(c + FF_LOOKAHEAD)
            act = conv_gelu(c)
            part = jnp.dot(act, wdown_ref[c * FF_CHUNK:(c + 1) * FF_CHUNK, :], preferred_element_type=F32)
            y = part if y is None else y + part
            yield
        _set_rows(y_ref, r0, sub, _rows(x_ref, r0, sub) + y)
        yield

    order = []
    for si in range(cfg.n_sub):
        body = [si] * (n_chunks + 1)
        if si + 1 < cfg.n_sub:
            body.insert(n_chunks - FF_LOOKAHEAD, si + 1)
        order += ([si] if si == 0 else []) + body
    _interleave([sub_tile(si) for si in range(cfg.n_sub)], order)
    cst_ref[...] = halo_s[...]


def _layer_spec(shape):
    return lambda layer: pl.BlockSpec((None,) + tuple(shape), lambda b, i: (layer,) + (0,) * len(shape))


def _hist_spec(ns, rows, width):
    return lambda layer: pl.BlockSpec((None, ns, rows, width), lambda b, i: (layer, b, 0, 0))


def _seq_spec(ns, rows, width):
    return pl.BlockSpec((ns, rows, width), lambda b, i: (b, 0, 0))


def _compiler_params():
    return pltpu.CompilerParams(dimension_semantics=("arbitrary", "arbitrary"),
                                vmem_limit_bytes=VMEM_LIMIT_BYTES)


def _mixer_call(cfg, layer, hist_layer, x, hist, prm):
    ns, rows, sub, qb, halo = cfg.n_seq, cfg.rows, cfg.sub_rows, cfg.q_block, cfg.kv_halo
    n_seq_total = cfg.n_groups * ns
    assert rows % sub == 0 and sub % qb == 0 and (ns == 1 or cfg.n_sub == 1)
    assert cfg.n_early_blocks * qb <= rows, "early blocks must sit in a sequence's first tile"
    assert cfg.keys_on_rows or rows <= CHUNK
    tile_spec = pl.BlockSpec((ns, rows, D_MODEL), lambda b, i: (b, i, 0))
    in_specs = [
        pl.BlockSpec(memory_space=pltpu.SMEM), tile_spec,
        _hist_spec(ns, WINDOW, KV_WIDTH)(hist_layer), _hist_spec(ns, WINDOW, KV_WIDTH)(hist_layer),
        _hist_spec(ns, POOL_HALO, POOL_WIDTH)(hist_layer),
    ] + [_layer_spec(prm[name].shape[1:])(layer) for name in
         ('norm_mix', 'w_in', 'q_gain', 'k_gain', 'w_pool', 'pool_scale', 'w_br_attn', 'w_br_pool',
          'gate_bias', 'w_out')]
    out_specs = [tile_spec, _seq_spec(ns, WINDOW, KV_WIDTH), _seq_spec(ns, WINDOW, KV_WIDTH),
                 _seq_spec(ns, POOL_HALO, POOL_WIDTH)]
    out_shape = [jax.ShapeDtypeStruct(x.shape, F32),
                 jax.ShapeDtypeStruct((n_seq_total, WINDOW, KV_WIDTH), F32),
                 jax.ShapeDtypeStruct((n_seq_total, WINDOW, KV_WIDTH), F32),
                 jax.ShapeDtypeStruct((n_seq_total, POOL_HALO, POOL_WIDTH), F32)]
    tokens = ns * rows
    if cfg.keys_on_rows:
        v_shape, a_shape = (N_KV_HEADS, ns, LANES, halo + rows), (ATTN_WIDTH, tokens)
        score_shape, sink_shape = (KEY_BLOCK, GROUP * qb), (SUBLANES, GROUP * qb)
    else:
        v_shape, a_shape = (N_KV_HEADS, ns, halo + rows, LANES), (tokens, ATTN_WIDTH)
        score_shape, sink_shape = (GROUP * qb, KEY_BLOCK), (GROUP * qb, LANES)
    scratch = [
        pltpu.VMEM((2, tokens, ATTN_WIDTH), BF16),
        pltpu.VMEM((N_KV_HEADS, ns, halo + rows, LANES), BF16),
        pltpu.VMEM(v_shape, BF16),
        pltpu.VMEM((ns, POOL_HALO + rows, POOL_WIDTH), F32),
        pltpu.VMEM(a_shape, BF16),
        pltpu.VMEM((2, tokens, D_MODEL), F32),
        pltpu.VMEM((1 + cfg.n_early_blocks, N_KV_HEADS) + score_shape, F32),
        pltpu.VMEM((N_KV_HEADS,) + sink_shape, F32),
    ]
    return pl.pallas_call(
        functools.partial(_mixer_kernel, cfg, layer),
        grid=(cfg.n_groups, cfg.n_tiles),
        in_specs=in_specs, out_specs=out_specs, out_shape=out_shape, scratch_shapes=scratch,
        compiler_params=_compiler_params(),
        name=f"mixer_r{rows}",
    )(prm['sinks'], x, hist['k'], hist['v'], hist['pool'], prm['norm_mix'], prm['w_in'], prm['q_gain'],
      prm['k_gain'], prm['w_pool'], prm['pool_scale'], prm['w_br_attn'], prm['w_br_pool'], prm['gate_bias'],
      prm['w_out'])


def _ffn_call(cfg, layer, hist_layer, x, hist, prm):
    ns, rows, sub = cfg.n_seq, cfg.rows, cfg.sub_rows
    n_seq_total = cfg.n_groups * ns
    tile_spec = pl.BlockSpec((ns, rows, D_MODEL), lambda b, i: (b, i, 0))
    in_specs = [tile_spec, _hist_spec(ns, CONV_HALO, 2 * D_FF)(hist_layer)] + [
        _layer_spec(prm[name].shape[1:])(layer) for name in ('norm_ffn', 'w_up', 'conv_w', 'conv_b', 'w_down')]
    out_specs = [tile_spec, _seq_spec(ns, CONV_HALO, 2 * D_FF)]
    out_shape = [jax.ShapeDtypeStruct(x.shape, F32),
                 jax.ShapeDtypeStruct((n_seq_total, CONV_HALO, 2 * D_FF), F32)]
    scratch = [
        pltpu.VMEM((N_CONV_BUFS, 2, ns, CONV_HALO + sub, FF_CHUNK), F32),
        pltpu.VMEM((ns, CONV_HALO, 2 * D_FF), F32),
    ]
    return pl.pallas_call(
        functools.partial(_ffn_kernel, cfg),
        grid=(cfg.n_groups, cfg.n_tiles),
        in_specs=in_specs, out_specs=out_specs, out_shape=out_shape, scratch_shapes=scratch,
        compiler_params=_compiler_params(),
        name=f"ffn_r{rows}",
    )(x, hist['conv'], prm['norm_ffn'], prm['w_up'], prm['conv_w'], prm['conv_b'], prm['w_down'])


def _layer(mixer_cfg, ffn_cfg, layer, hist_layer, x, hist, prm):
    x, k_state, v_state, p_state = _mixer_call(mixer_cfg, layer, hist_layer, x, hist, prm)
    x, c_state = _ffn_call(ffn_cfg, layer, hist_layer, x, hist, prm)
    return x, k_state, v_state, p_state[:, POOL_HALO - POOL_HIST:], c_state[:, CONV_HALO - CONV_HIST:]


def kernel(x_prompt, x_sample, cache_k, cache_v, state_pool, state_conv, norm_mix, w_in, q_norm, k_norm,
           sinks, w_pool, pool_scale, w_br_attn, w_br_pool, gate_bias, w_out, norm_ffn, w_up, conv_w,
           conv_b, w_down):
    depth = w_in.shape[0]
    batch, seq = x_prompt.shape[0], x_prompt.shape[1]
    dec_batch, dec_seq = x_sample.shape[0], x_sample.shape[1]
    assert seq % MIXER_TILE == 0 and seq % FFN_TILE == 0 and dec_seq % SUBLANES == 0 and dec_seq <= CHUNK
    prompt_cfgs = [TileCfg(n_groups=batch, n_seq=1, rows=t, n_tiles=seq // t, pos0=0) for t in (MIXER_TILE, FFN_TILE)]
    sample_cfg = TileCfg(n_groups=1, n_seq=dec_batch, rows=dec_seq, n_tiles=1, pos0=PAST_LEN)

    row = lambda a: a[:, None, :]
    prm = {
        'sinks': sinks,
        'norm_mix': row(norm_mix), 'w_in': w_in.astype(BF16),
        'q_gain': row(jnp.tile(q_norm, (1, 2)) * HEAD_DIM ** -0.5), 'k_gain': row(jnp.tile(k_norm, (1, 2))),
        'w_pool': w_pool.astype(BF16), 'pool_scale': row(pool_scale),
        'w_br_attn': w_br_attn.astype(BF16), 'w_br_pool': w_br_pool.astype(BF16),
        'gate_bias': gate_bias, 'w_out': w_out.astype(BF16),
        'norm_ffn': row(norm_ffn), 'w_up': w_up.astype(BF16), 'conv_w': conv_w,
        'conv_b': row(conv_b), 'w_down': w_down.astype(BF16),
    }
    prompt_hist = {
        'k': jnp.zeros((1, batch, WINDOW, KV_WIDTH), F32), 'v': jnp.zeros((1, batch, WINDOW, KV_WIDTH), F32),
        'pool': jnp.zeros((1, batch, POOL_HALO, POOL_WIDTH), F32),
        'conv': jnp.zeros((1, batch, CONV_HALO, 2 * D_FF), F32),
    }
    sample_hist = {
        'k': cache_k.reshape(depth, dec_batch, WINDOW, KV_WIDTH),
        'v': cache_v.reshape(depth, dec_batch, WINDOW, KV_WIDTH),
        'pool': jnp.pad(state_pool, ((0, 0), (0, 0), (POOL_HALO - POOL_HIST, 0), (0, 0))),
        'conv': jnp.pad(state_conv, ((0, 0), (0, 0), (CONV_HALO - CONV_HIST, 0), (0, 0))),
    }

    xp, xs = x_prompt, x_sample
    states_p, states_s = [], []
    for l in range(depth):
        xp, *st_p = _layer(*prompt_cfgs, l, 0, xp, prompt_hist, prm)
        xs, *st_s = _layer(sample_cfg, sample_cfg, l, l, xs, sample_hist, prm)
        states_p.append(st_p)
        states_s.append(st_s)

    def stacked(states, idx, n_seq):
        arr = jnp.stack([st[idx] for st in states], axis=0)
        if idx < 2:
            arr = arr.reshape(depth, n_seq, WINDOW, N_KV_HEADS, HEAD_DIM)
        return arr

    return (xp, xs,
            stacked(states_p, 0, batch), stacked(states_p, 1, batch),
            stacked(states_p, 2, batch), stacked(states_p, 3, batch),
            stacked(states_s, 0, dec_batch), stacked(states_s, 1, dec_batch),
            stacked(states_s, 2, dec_batch), stacked(states_s, 3, dec_batch))
```
